```python
import math
import jax
import jax.numpy as jnp
from jax import lax
import numpy as np

D_MODEL = 2048
BATCH = 2
SEQ = 8192
DEPTH = 4
DEC_BATCH = 32
DEC_SEQ = 16
PAST_LEN = 2048

CHUNK = 64
N_MIXERS = 4
N_POOL_LAYERS = len(range(0, DEPTH, N_MIXERS))
N_SCONV_LAYERS = len(range(1, DEPTH, N_MIXERS))
N_DSA_LAYERS = len(range(2, DEPTH, N_MIXERS))
N_CCONV_LAYERS = len(range(3, DEPTH, N_MIXERS))
D_FF = 5632
NORM_EPS = 1e-6
POOL_WINDOWS = (2, 4, 8, 16)
POOL_MAX = 16
POOL_GROUP = D_MODEL // 4
SCONV_WIDTH = 3
N_HEADS = 16
HEAD_DIM = D_MODEL // N_HEADS
KV_HEADS = 4
ROPE_DIM = HEAD_DIM // 4
ROPE_THETA = 500000.0
IDX_HEADS = 16
IDX_DIM = 64
IDX_ROPE_DIM = IDX_DIM // 4
IDX_SCALE = IDX_HEADS ** -0.5 * IDX_DIM ** -0.5
TOPK = 256
Q_BLOCK = 128
CCONV_WIDTH = 31
Q_W = N_HEADS * HEAD_DIM
KV_W = KV_HEADS * HEAD_DIM
IDXQ_W = IDX_HEADS * IDX_DIM
ATTN_SPLITS = (Q_W, Q_W + KV_W, Q_W + 2 * KV_W, Q_W + 2 * KV_W + IDXQ_W,
               Q_W + 2 * KV_W + IDXQ_W + IDX_DIM)
ATTN_IN = Q_W + 2 * KV_W + IDXQ_W + IDX_DIM + IDX_HEADS

kernel_name = "hybrid_streaming_encoder_step"


def rms_norm(x, g):
    xf = x.astype(jnp.float32)
    y = xf * lax.rsqrt(jnp.mean(xf * xf, axis=-1, keepdims=True) + NORM_EPS)
    return (y * g.astype(jnp.float32)).astype(x.dtype)


def layer_norm(x, g, b):
    xf = x.astype(jnp.float32)
    mu = jnp.mean(xf, axis=-1, keepdims=True)
    xc = xf - mu
    y = xc * lax.rsqrt(jnp.mean(xc * xc, axis=-1, keepdims=True) + NORM_EPS)
    return (y * g.astype(jnp.float32) + b.astype(jnp.float32)).astype(x.dtype)


def modulate(h, shift, scale):
    return h * (1 + scale[:, None, :]) + shift[:, None, :]


def swiglu(h, w_gate, w_up, w_down):
    return (jax.nn.silu(h @ w_gate) * (h @ w_up)) @ w_down


def rope(x, pos, rot_dim):
    half = rot_dim // 2
    inv = jnp.exp(-math.log(ROPE_THETA) * jnp.arange(half, dtype=jnp.float32) * (2.0 / rot_dim))
    ang = pos.astype(jnp.float32)[:, None] * inv[None, :]
    cos = jnp.cos(ang)[None, :, None, :].astype(x.dtype)
    sin = jnp.sin(ang)[None, :, None, :].astype(x.dtype)
    x1, x2 = x[..., :half], x[..., half:rot_dim]
    return jnp.concatenate([x1 * cos - x2 * sin, x2 * cos + x1 * sin, x[..., rot_dim:]], axis=-1)


def causal_dwconv(u, prefix, w):
    u_ext = jnp.concatenate([prefix.astype(u.dtype), u], axis=1)
    y = lax.conv_general_dilated(u_ext, w.astype(u.dtype)[:, None, :], window_strides=(1,),
                                 padding='VALID', dimension_numbers=('NWC', 'WIO', 'NWC'),
                                 feature_group_count=u.shape[-1])
    return y, u_ext[:, u_ext.shape[1] - (w.shape[0] - 1):]


def pool_mixer(h, prefix, n_valid_prefix, w_pool, s_pool):
    B, T, D = h.shape
    P = POOL_MAX - 1
    h_ext = jnp.concatenate([prefix.astype(h.dtype), h], axis=1)
    cs = jnp.cumsum(h_ext.astype(jnp.float32), axis=1)
    cs = jnp.concatenate([jnp.zeros((B, 1, D), jnp.float32), cs], axis=1)
    cur = cs[:, P + 1:P + 1 + T]
    hf = h.astype(jnp.float32)
    t = jnp.arange(T)
    groups = []
    for g, w in enumerate(POOL_WINDOWS):
        lo, hi = g * POOL_GROUP, (g + 1) * POOL_GROUP
        win = cur[..., lo:hi] - cs[:, P + 1 - w:P + 1 - w + T, lo:hi]
        cnt = jnp.minimum(t + 1 + n_valid_prefix, w).astype(jnp.float32)
        groups.append(win / cnt[None, :, None] - hf[..., lo:hi])
    pooled = jnp.stack(groups, axis=2).astype(h.dtype)
    y = jnp.einsum('btgc,gce->btge', pooled, w_pool).reshape(B, T, D) * s_pool
    return y, h_ext[:, h_ext.shape[1] - P:]


def sconv_mixer(h, prefix, w_in, w_conv, w_out):
    b_gate, c_gate, v = jnp.split(h @ w_in, 3, axis=-1)
    conv, st = causal_dwconv(c_gate * v, prefix, w_conv)
    return (b_gate * conv) @ w_out, st


def cconv_mixer(h, prefix, w_pw1, b_pw1, w_dw, b_dw, g_ln, b_ln, w_pw2, b_pw2):
    a, gate = jnp.split(h @ w_pw1 + b_pw1, 2, axis=-1)
    u = a * jax.nn.sigmoid(gate)
    conv, st = causal_dwconv(u, prefix, w_dw)
    n = layer_norm(conv + b_dw, g_ln, b_ln)
    return jax.nn.silu(n) @ w_pw2 + b_pw2, st


def dsa_mixer(h, pos0, past, w_in, w_out):
    B, T, _ = h.shape
    G = N_HEADS // KV_HEADS
    q, k, v, qi, ki, wi = jnp.split(h @ w_in, ATTN_SPLITS, axis=-1)
    pos = pos0 + jnp.arange(T)
    q = rope(q.reshape(B, T, N_HEADS, HEAD_DIM), pos, ROPE_DIM)
    k = rope(k.reshape(B, T, KV_HEADS, HEAD_DIM), pos, ROPE_DIM)
    v = v.reshape(B, T, KV_HEADS, HEAD_DIM)
    qi = rope(qi.reshape(B, T, IDX_HEADS, IDX_DIM), pos, IDX_ROPE_DIM)
    ki = rope(ki[:, :, None, :], pos, IDX_ROPE_DIM)[:, :, 0]
    if past is None:
        K, V, KI = k, v, ki
    else:
        K = jnp.concatenate([past[0], k], axis=1)
        V = jnp.concatenate([past[1], v], axis=1)
        KI = jnp.concatenate([past[2], ki], axis=1)
    L = K.shape[1]
    n_top = min(TOPK, L // 4)
    key_chunk = jnp.arange(L) // CHUNK
    QB = min(Q_BLOCK, T)
    nb = T // QB

    def to_blocks(a):
        return a.reshape((B, nb, QB) + a.shape[2:]).swapaxes(0, 1)

    def attend_block(blk):
        qb, qib, wib, qcb = blk
        s_idx = jnp.einsum('bqhd,bsd->bqhs', qib, KI, preferred_element_type=jnp.float32)
        score = jnp.einsum('bqhs,bqh->bqs', jax.nn.relu(s_idx), wib.astype(jnp.float32)) * IDX_SCALE
        adm = key_chunk[None, None, :] <= qcb[None, :, None]
        score = jnp.where(adm, score, -jnp.inf)
        _, sel = lax.top_k(score, n_top)
        k_sel = jax.vmap(lambda kb, ib: kb[ib])(K, sel)
        v_sel = jax.vmap(lambda vb, ib: vb[ib])(V, sel)
        valid = key_chunk[sel] <= qcb[None, :, None]
        qg = qb.reshape(B, QB, KV_HEADS, G, HEAD_DIM)
        s = jnp.einsum('bqgrd,bqngd->bqgrn', qg, k_sel,
                       preferred_element_type=jnp.float32) * (HEAD_DIM ** -0.5)
        s = jnp.where(valid[:, :, None, None, :], s, -jnp.inf)
        p = jax.nn.softmax(s, axis=-1).astype(v_sel.dtype)
        return jnp.einsum('bqgrn,bqngd->bqgrd', p, v_sel).reshape(B, QB, N_HEADS * HEAD_DIM)

    q_chunk = (pos // CHUNK).reshape(nb, QB)
    o = lax.map(attend_block, (to_blocks(q), to_blocks(qi), to_blocks(wi), q_chunk))
    o = o.swapaxes(0, 1).reshape(B, T, N_HEADS * HEAD_DIM)
    return o @ w_out, (k, v, ki)


def trunk(x, c, pos0, past_pool, past_sconv, past_k, past_v, past_kidx, past_cconv,
          w_mod, b_mod, g_norm, w_ffn_gate, w_ffn_up, w_ffn_down, w_pool, s_pool,
          w_sc_in, w_sc_conv, w_sc_out, w_attn_in, w_attn_out,
          w_cm_pw1, b_cm_pw1, w_cm_dw, b_cm_dw, g_cm_ln, b_cm_ln, w_cm_pw2, b_cm_pw2, g_final):
    B, T, D = x.shape
    fresh = past_k is None
    c_act = jax.nn.silu(c)
    new_pool, new_sconv, new_k, new_v, new_kidx, new_cconv = [], [], [], [], [], []
    for i in range(DEPTH):
        kind, j = i % N_MIXERS, i // N_MIXERS
        mod = (c_act @ w_mod[i] + b_mod[i]).reshape(B, 3, 3, D)
        h = modulate(rms_norm(x, g_norm[i, 0]), mod[:, 0, 0], mod[:, 0, 1])
        x = x + 0.5 * mod[:, 0, 2][:, None] * swiglu(h, w_ffn_gate[i, 0], w_ffn_up[i, 0], w_ffn_down[i, 0])
        h = modulate(rms_norm(x, g_norm[i, 1]), mod[:, 1, 0], mod[:, 1, 1])
        if kind == 0:
            prefix = jnp.zeros((B, POOL_MAX - 1, D), h.dtype) if fresh else past_pool[j]
            y, st = pool_mixer(h, prefix, 0 if fresh else POOL_MAX - 1, w_pool[j], s_pool[j])
            new_pool.append(st)
        elif kind == 1:
            prefix = jnp.zeros((B, SCONV_WIDTH - 1, D), h.dtype) if fresh else past_sconv[j]
            y, st = sconv_mixer(h, prefix, w_sc_in[j], w_sc_conv[j], w_sc_out[j])
            new_sconv.append(st)
        elif kind == 2:
            past = None if fresh else (past_k[j], past_v[j], past_kidx[j])
            y, (kk, vv, kix) = dsa_mixer(h, pos0, past, w_attn_in[j], w_attn_out[j])
            new_k.append(kk)
            new_v.append(vv)
            new_kidx.append(kix)
        else:
            prefix = jnp.zeros((B, CCONV_WIDTH - 1, D), h.dtype) if fresh else past_cconv[j]
            y, st = cconv_mixer(h, prefix, w_cm_pw1[j], b_cm_pw1[j], w_cm_dw[j], b_cm_dw[j],
                                g_cm_ln[j], b_cm_ln[j], w_cm_pw2[j], b_cm_pw2[j])
            new_cconv.append(st)
        x = x + mod[:, 1, 2][:, None] * y
        h = modulate(rms_norm(x, g_norm[i, 2]), mod[:, 2, 0], mod[:, 2, 1])
        x = x + 0.5 * mod[:, 2, 2][:, None] * swiglu(h, w_ffn_gate[i, 1], w_ffn_up[i, 1], w_ffn_down[i, 1])
    y_out = rms_norm(x, g_final)
    return y_out, (jnp.stack(new_pool), jnp.stack(new_sconv), jnp.stack(new_k), jnp.stack(new_v),
                   jnp.stack(new_kidx), jnp.stack(new_cconv))


def setup_inputs(seed: int = 0) -> dict:
    key = jax.random.key(seed)
    ks = jax.random.split(key, 32)
    D = D_MODEL

    def nrm(k, shape, scale):
        return scale * jax.random.normal(k, shape, jnp.float32)

    return {
        "x_prompt": nrm(ks[0], (BATCH, SEQ, D), 1.0),
        "x_sample": nrm(ks[1], (DEC_BATCH, DEC_SEQ, D), 1.0),
        "state_pool": nrm(ks[2], (N_POOL_LAYERS, DEC_BATCH, POOL_MAX - 1, D), 1.0),
        "state_sconv": nrm(ks[3], (N_SCONV_LAYERS, DEC_BATCH, SCONV_WIDTH - 1, D), 1.0),
        "cache_k": nrm(ks[4], (N_DSA_LAYERS, DEC_BATCH, PAST_LEN, KV_HEADS, HEAD_DIM), 1.0),
        "cache_v": nrm(ks[5], (N_DSA_LAYERS, DEC_BATCH, PAST_LEN, KV_HEADS, HEAD_DIM), 1.0),
        "cache_kidx": nrm(ks[6], (N_DSA_LAYERS, DEC_BATCH, PAST_LEN, IDX_DIM), 1.0),
        "state_cconv": nrm(ks[7], (N_CCONV_LAYERS, DEC_BATCH, CCONV_WIDTH - 1, D), 1.0),
        "c_prompt": nrm(ks[8], (BATCH, D), 1.0),
        "c_sample": nrm(ks[9], (DEC_BATCH, D), 1.0),
        "w_mod": nrm(ks[10], (DEPTH, D, 9 * D), 0.5 * D ** -0.5),
        "b_mod": nrm(ks[11], (DEPTH, 9 * D), 0.01),
        "g_norm": 1.0 + nrm(ks[12], (DEPTH, 3, D), 0.01),
        "w_ffn_gate": nrm(ks[13], (DEPTH, 2, D, D_FF), D ** -0.5),
        "w_ffn_up": nrm(ks[14], (DEPTH, 2, D, D_FF), D ** -0.5),
        "w_ffn_down": nrm(ks[15], (DEPTH, 2, D_FF, D), D_FF ** -0.5),
        "w_pool": nrm(ks[16], (N_POOL_LAYERS, 4, POOL_GROUP, POOL_GROUP), POOL_GROUP ** -0.5),
        "s_pool": 1.0 + nrm(ks[17], (N_POOL_LAYERS, D), 0.1),
        "w_sc_in": nrm(ks[18], (N_SCONV_LAYERS, D, 3 * D), D ** -0.5),
        "w_sc_conv": nrm(ks[19], (N_SCONV_LAYERS, SCONV_WIDTH, D), SCONV_WIDTH ** -0.5),
        "w_sc_out": nrm(ks[20], (N_SCONV_LAYERS, D, D), D ** -0.5),
        "w_attn_in": nrm(ks[21], (N_DSA_LAYERS, D, ATTN_IN), D ** -0.5),
        "w_attn_out": nrm(ks[22], (N_DSA_LAYERS, Q_W, D), Q_W ** -0.5),
        "w_cm_pw1": nrm(ks[23], (N_CCONV_LAYERS, D, 2 * D), D ** -0.5),
        "b_cm_pw1": nrm(ks[24], (N_CCONV_LAYERS, 2 * D), 0.01),
        "w_cm_dw": nrm(ks[25], (N_CCONV_LAYERS, CCONV_WIDTH, D), CCONV_WIDTH ** -0.5),
        "b_cm_dw": nrm(ks[26], (N_CCONV_LAYERS, D), 0.01),
        "g_cm_ln": 1.0 + nrm(ks[27], (N_CCONV_LAYERS, D), 0.01),
        "b_cm_ln": nrm(ks[28], (N_CCONV_LAYERS, D), 0.01),
        "w_cm_pw2": nrm(ks[29], (N_CCONV_LAYERS, D, D), D ** -0.5),
        "b_cm_pw2": nrm(ks[30], (N_CCONV_LAYERS, D), 0.01),
        "g_final": 1.0 + nrm(ks[31], (D,), 0.01),
    }


def reference(x_prompt, x_sample, state_pool, state_sconv, cache_k, cache_v, cache_kidx, state_cconv,
              c_prompt, c_sample, w_mod, b_mod, g_norm, w_ffn_gate, w_ffn_up, w_ffn_down,
              w_pool, s_pool, w_sc_in, w_sc_conv, w_sc_out, w_attn_in, w_attn_out,
              w_cm_pw1, b_cm_pw1, w_cm_dw, b_cm_dw, g_cm_ln, b_cm_ln, w_cm_pw2, b_cm_pw2, g_final):
    weights = (w_mod, b_mod, g_norm, w_ffn_gate, w_ffn_up, w_ffn_down, w_pool, s_pool,
               w_sc_in, w_sc_conv, w_sc_out, w_attn_in, w_attn_out,
               w_cm_pw1, b_cm_pw1, w_cm_dw, b_cm_dw, g_cm_ln, b_cm_ln, w_cm_pw2, b_cm_pw2, g_final)
    y_prompt, st_p = trunk(x_prompt, c_prompt, 0, None, None, None, None, None, None, *weights)
    pool_p, sconv_p, k_p, v_p, kidx_p, cconv_p = st_p
    y_sample, st_s = trunk(x_sample, c_sample, PAST_LEN, state_pool, state_sconv, cache_k, cache_v,
                           cache_kidx, state_cconv, *weights)
    pool_s, sconv_s, k_s, v_s, kidx_s, cconv_s = st_s
    return (y_prompt, y_sample, pool_p, pool_s, sconv_p, sconv_s, k_p, k_s, v_p, v_s,
            kidx_p, kidx_s, cconv_p, cconv_s)
```

```python
import functools
import math

import jax
import jax.numpy as jnp
import numpy as np
from jax import lax
from jax.experimental import pallas as pl
from jax.experimental.pallas import tpu as pltpu

D = 2048
D_FF = 5632
DEPTH = 4
NORM_EPS = 1e-6
CHUNK = 64
POOL_WINDOWS = (2, 4, 8, 16)
POOL_GROUP = D // 4
POOL_PAD = 16
SCONV_WIDTH = 3
SCONV_PAD = 8
CCONV_WIDTH = 31
CCONV_PAD = 32
N_HEADS = 16
HEAD_DIM = 128
KV_HEADS = 4
HEAD_GROUP = N_HEADS // KV_HEADS
ROPE_DIM = HEAD_DIM // 4
ROPE_THETA = 500000.0
IDX_HEADS = 16
IDX_DIM = 64
IDX_ROPE_DIM = IDX_DIM // 4
IDX_SCALE = IDX_HEADS ** -0.5 * IDX_DIM ** -0.5
TOPK = 256
Q_W = N_HEADS * HEAD_DIM
KV_W = KV_HEADS * HEAD_DIM
IDXQ_W = IDX_HEADS * IDX_DIM
ATTN_MAIN = Q_W + 2 * KV_W + IDXQ_W
ATTN_TAIL = IDX_DIM + IDX_HEADS
LANE = 128
MOD_ROWS = 40
INT_MIN = -2 ** 31
NEG_BIG = -1e30

BF = jnp.bfloat16
F32 = jnp.float32


def _cparams(sem, vmem_mib):
    return pltpu.CompilerParams(dimension_semantics=sem, vmem_limit_bytes=vmem_mib << 20)


def _rms(x, g):
    return x * lax.rsqrt(jnp.mean(x * x, axis=-1, keepdims=True) + NORM_EPS) * g


def _rms_mod(x, g, shift, scale):
    return _rms(x, g) * (1.0 + scale) + shift


def _silu(x):
    return x * jax.nn.sigmoid(x)


def _row_spec(S, R, W):
    return pl.BlockSpec((S, R, W), lambda b, t, n: (b, t, 0))


def _mod_spec(S, col):
    return pl.BlockSpec((S, 1, D), lambda b, t, n: (b, 0, col))


def _const_spec(shape):
    nd = len(shape)
    return pl.BlockSpec(shape, lambda b, t, n: (0,) * nd)


def _mod_kernel(c_ref, w_ref, b_ref, o_ref):
    a = _silu(c_ref[...]).astype(BF)
    o_ref[0] = jnp.dot(a, w_ref[0].astype(BF), preferred_element_type=F32) + b_ref[0]


def _mod_call(c_all, w_mod, b_mod):
    TN = 1024
    N = 9 * D
    return pl.pallas_call(
        _mod_kernel,
        grid=(DEPTH, N // TN),
        in_specs=[pl.BlockSpec((MOD_ROWS, D), lambda i, n: (0, 0)),
                  pl.BlockSpec((1, D, TN), lambda i, n: (i, 0, n)),
                  pl.BlockSpec((1, 1, TN), lambda i, n: (i, 0, n))],
        out_specs=pl.BlockSpec((1, MOD_ROWS, TN), lambda i, n: (i, 0, n)),
        out_shape=jax.ShapeDtypeStruct((DEPTH, MOD_ROWS, N), F32),
        compiler_params=_cparams(("arbitrary", "arbitrary"), 40),
        name="mod",
    )(c_all, w_mod, b_mod.reshape(DEPTH, 1, N))


def _ffn_kernel(*refs, S, R, final):
    if final:
        x_ref, sh_ref, sc_ref, gt_ref, g_ref, wg_ref, wu_ref, wd_ref, gf_ref, o_ref, h_scr = refs
    else:
        x_ref, sh_ref, sc_ref, gt_ref, g_ref, wg_ref, wu_ref, wd_ref, o_ref, h_scr = refs
    f = pl.program_id(2)
    M = S * R

    @pl.when(f == 0)
    def _():
        h = _rms_mod(x_ref[...], g_ref[...], sh_ref[...], sc_ref[...])
        h_scr[...] = h.reshape(M, D).astype(BF)
        o_ref[...] = jnp.zeros_like(o_ref)

    h = h_scr[...]
    g = jnp.dot(h, wg_ref[...], preferred_element_type=F32)
    u = jnp.dot(h, wu_ref[...], preferred_element_type=F32)
    a = (_silu(g) * u).astype(BF)
    o_ref[...] += jnp.dot(a, wd_ref[...], preferred_element_type=F32).reshape(S, R, D)

    @pl.when(f == pl.num_programs(2) - 1)
    def _():
        y = x_ref[...] + 0.5 * gt_ref[...] * o_ref[...]
        if final:
            y = _rms(y, gf_ref[...])
        o_ref[...] = y


def _ffn_call(x, mod, g, wg, wu, wd, layer, which, sub, S, R, TF, gfin=None):
    NB, TT, _ = x.shape
    final = gfin is not None
    in_specs = [_row_spec(S, R, D), _mod_spec(S, 3 * sub), _mod_spec(S, 3 * sub + 1),
                _mod_spec(S, 3 * sub + 2), _const_spec((1, D)),
                pl.BlockSpec((None, None, D, TF), lambda b, t, f: (layer, which, 0, f)),
                pl.BlockSpec((None, None, D, TF), lambda b, t, f: (layer, which, 0, f)),
                pl.BlockSpec((None, None, TF, D), lambda b, t, f: (layer, which, f, 0))]
    args = [x, mod, mod, mod, g, wg, wu, wd]
    if final:
        in_specs.append(_const_spec((1, D)))
        args.append(gfin)
    return pl.pallas_call(
        functools.partial(_ffn_kernel, S=S, R=R, final=final),
        grid=(NB // S, TT // R, D_FF // TF),
        in_specs=in_specs,
        out_specs=_row_spec(S, R, D),
        out_shape=jax.ShapeDtypeStruct(x.shape, F32),
        scratch_shapes=[pltpu.VMEM((S * R, D), BF)],
        compiler_params=_cparams(("arbitrary", "arbitrary", "arbitrary"), 48),
        name="ffn",
    )(*args)


def _linres_kernel(*refs, S, R, ln, bias):
    refs = list(refs)
    a_ref, x_ref, gt_ref, w_ref = refs[:4]
    rest = refs[4:]
    if ln:
        gl_ref, bl_ref = rest[:2]
        rest = rest[2:]
    if bias:
        b_ref = rest[0]
        rest = rest[1:]
    o_ref, a_scr = rest
    n = pl.program_id(2)
    M = S * R

    @pl.when(n == 0)
    def _():
        a = a_ref[...]
        if ln:
            a = a.astype(F32)
            mu = jnp.mean(a, axis=-1, keepdims=True)
            ac = a - mu
            a = ac * lax.rsqrt(jnp.mean(ac * ac, axis=-1, keepdims=True) + NORM_EPS)
            a = _silu(a * gl_ref[...] + bl_ref[...])
        a_scr[...] = a.reshape(M, a.shape[-1]).astype(BF)

    y = jnp.dot(a_scr[...], w_ref[...], preferred_element_type=F32)
    if bias:
        y = y + b_ref[...]
    o_ref[...] = x_ref[...] + gt_ref[...] * y.reshape(o_ref.shape)


def _linres_call(a, x, mod, gate_col, w, S, R, TN, ln=None, bias=None):
    NB, TT, K = a.shape
    in_specs = [_row_spec(S, R, K),
                pl.BlockSpec((S, R, TN), lambda b, t, n: (b, t, n)),
                pl.BlockSpec((S, 1, TN), lambda b, t, n: (b, 0, gate_col * (D // TN) + n)),
                pl.BlockSpec((K, TN), lambda b, t, n: (0, n))]
    args = [a, x, mod, w]
    if ln is not None:
        in_specs += [_const_spec((1, K)), _const_spec((1, K))]
        args += list(ln)
    if bias is not None:
        in_specs.append(pl.BlockSpec((1, TN), lambda b, t, n: (0, n)))
        args.append(bias)
    return pl.pallas_call(
        functools.partial(_linres_kernel, S=S, R=R, ln=ln is not None, bias=bias is not None),
        grid=(NB // S, TT // R, D // TN),
        in_specs=in_specs,
        out_specs=pl.BlockSpec((S, R, TN), lambda b, t, n: (b, t, n)),
        out_shape=jax.ShapeDtypeStruct(x.shape, F32),
        scratch_shapes=[pltpu.VMEM((S * R, K), BF)],
        compiler_params=_cparams(("arbitrary", "arbitrary", "arbitrary"), 40),
        name="linres",
    )(*args)


def _pool_kernel(*refs, R, has_prefix, n_valid_prefix):
    if has_prefix:
        (x_ref, sh_ref, sc_ref, gt_ref, g_ref, wp_ref, sp_ref, pre_ref,
         o_ref, st_ref, hext) = refs
    else:
        (x_ref, sh_ref, sc_ref, gt_ref, g_ref, wp_ref, sp_ref,
         o_ref, st_ref, hext, carry) = refs
    t = pl.program_id(1)
    x = x_ref[0]
    h = _rms_mod(x, g_ref[...], sh_ref[0], sc_ref[0])
    if has_prefix:
        hext[0:POOL_PAD, :] = pre_ref[0]
    else:
        @pl.when(t == 0)
        def _():
            hext[0:POOL_PAD, :] = jnp.zeros((POOL_PAD, D), F32)

        @pl.when(t > 0)
        def _():
            hext[0:POOL_PAD, :] = carry[...]

        carry[...] = h[R - POOL_PAD:R, :]
    hext[POOL_PAD:POOL_PAD + R, :] = h
    st_ref[0] = h[R - POOL_PAD:R, :]
    tpos = t * R + lax.broadcasted_iota(jnp.int32, (R, 1), 0)
    gate = gt_ref[0]
    for gi, w in enumerate(POOL_WINDOWS):
        lo = gi * POOL_GROUP
        hi = lo + POOL_GROUP
        win = h[:, lo:hi]
        for j in range(1, w):
            win = win + hext[POOL_PAD - j:POOL_PAD - j + R, lo:hi]
        cnt = jnp.minimum(tpos + 1 + n_valid_prefix, w).astype(F32)
        pooled = (win / cnt - h[:, lo:hi]).astype(BF)
        y = jnp.dot(pooled, wp_ref[gi], preferred_element_type=F32) * sp_ref[:, lo:hi]
        o_ref[0, :, lo:hi] = x[:, lo:hi] + gate[:, lo:hi] * y


def _pool_call(x, mod, g, w_pool, s_pool, R, prefix=None):
    NB, TT, _ = x.shape
    has_prefix = prefix is not None
    in_specs = [_row_spec(1, R, D), _mod_spec(1, 3), _mod_spec(1, 4), _mod_spec(1, 5),
                _const_spec((1, D)), _const_spec((4, POOL_GROUP, POOL_GROUP)), _const_spec((1, D))]
    args = [x, mod, mod, mod, g, w_pool, s_pool]
    scratch = [pltpu.VMEM((R + POOL_PAD, D), F32)]
    if has_prefix:
        in_specs.append(pl.BlockSpec((1, POOL_PAD, D), lambda b, t, n: (b, 0, 0)))
        args.append(prefix)
    else:
        scratch.append(pltpu.VMEM((POOL_PAD, D), F32))
    return pl.pallas_call(
        functools.partial(_pool_kernel, R=R, has_prefix=has_prefix,
                          n_valid_prefix=POOL_PAD - 1 if has_prefix else 0),
        grid=(NB, TT // R, 1),
        in_specs=in_specs,
        out_specs=[_row_spec(1, R, D), pl.BlockSpec((1, POOL_PAD, D), lambda b, t, n: (b, 0, 0))],
        out_shape=[jax.ShapeDtypeStruct(x.shape, F32), jax.ShapeDtypeStruct((NB, POOL_PAD, D), F32)],
        scratch_shapes=scratch,
        compiler_params=_cparams(("arbitrary", "arbitrary", "arbitrary"), 40),
        name="pool",
    )(*args)


def _conv_kernel(*refs, S, R, TN, mode, has_prefix):
    refs = list(refs)
    x_ref, sh_ref, sc_ref, g_ref = refs[:4]
    rest = refs[4:]
    nproj = 3 if mode == "sconv" else 2
    w_refs = rest[:nproj]
    rest = rest[nproj:]
    if mode == "cconv":
        pb_refs = rest[:2]
        wc_ref, bdw_ref = rest[2:4]
        rest = rest[4:]
        width, pad = CCONV_WIDTH, CCONV_PAD
    else:
        wc_ref = rest[0]
        rest = rest[1:]
        width, pad = SCONV_WIDTH, SCONV_PAD
    if has_prefix:
        pre_ref = rest[0]
        z_ref, st_ref, h_scr, uext = rest[1:]
    else:
        z_ref, st_ref, h_scr, uext, carry = rest
    t = pl.program_id(1)
    n = pl.program_id(2)
    M = S * R

    @pl.when(n == 0)
    def _():
        h = _rms_mod(x_ref[...], g_ref[...], sh_ref[...], sc_ref[...])
        h_scr[...] = h.reshape(M, D).astype(BF)

    h = h_scr[...]
    proj = [jnp.dot(h, w_ref[...], preferred_element_type=F32) for w_ref in w_refs]
    if mode == "sconv":
        bq, cq, vq = proj
        u = cq * vq
    else:
        a = proj[0] + pb_refs[0][...]
        gq = proj[1] + pb_refs[1][...]
        u = a * jax.nn.sigmoid(gq)
    u3 = u.reshape(S, R, TN)
    if has_prefix:
        uext[:, 0:pad, :] = pre_ref[...]
    else:
        @pl.when(t == 0)
        def _():
            uext[:, 0:pad, :] = jnp.zeros((S, pad, TN), F32)

        @pl.when(t > 0)
        def _():
            uext[:, 0:pad, :] = carry[n]
    uext[:, pad:pad + R, :] = u3
    wc = wc_ref[...]
    conv = wc[width - 1:width, :] * u3
    for j in range(width - 1):
        off = pad - (width - 1) + j
        conv = conv + wc[j:j + 1, :] * uext[:, off:off + R, :]
    tail = uext[:, R:R + pad, :]
    if not has_prefix:
        carry[n] = tail
    st_ref[...] = tail
    if mode == "sconv":
        z_ref[...] = (bq.reshape(S, R, TN) * conv).astype(z_ref.dtype)
    else:
        z_ref[...] = conv + bdw_ref[...]


def _conv_call(x, mod, g, w, mode, S, R, TN, w_conv, pbias=None, b_dw=None, prefix=None):
    NB, TT, _ = x.shape
    has_prefix = prefix is not None
    nproj = 3 if mode == "sconv" else 2
    width, pad = (SCONV_WIDTH, SCONV_PAD) if mode == "sconv" else (CCONV_WIDTH, CCONV_PAD)
    nN = D // TN
    in_specs = [_row_spec(S, R, D), _mod_spec(S, 3), _mod_spec(S, 4), _const_spec((1, D))]
    args = [x, mod, mod, g]
    for k in range(nproj):
        in_specs.append(pl.BlockSpec((D, TN), lambda b, t, n, k=k: (0, k * nN + n)))
        args.append(w)
    if mode == "cconv":
        for k in range(2):
            in_specs.append(pl.BlockSpec((1, TN), lambda b, t, n, k=k: (0, k * nN + n)))
            args.append(pbias)
    in_specs.append(pl.BlockSpec((width, TN), lambda b, t, n: (0, n)))
    args.append(w_conv)
    if mode == "cconv":
        in_specs.append(pl.BlockSpec((1, TN), lambda b, t, n: (0, n)))
        args.append(b_dw)
    scratch = [pltpu.VMEM((S * R, D), BF), pltpu.VMEM((S, R + pad, TN), F32)]
    if has_prefix:
        in_specs.append(pl.BlockSpec((S, pad, TN), lambda b, t, n: (b, 0, n)))
        args.append(prefix)
    else:
        scratch.append(pltpu.VMEM((nN, S, pad, TN), F32))
    zdt = BF if mode == "sconv" else F32
    z, tails = pl.pallas_call(
        functools.partial(_conv_kernel, S=S, R=R, TN=TN, mode=mode, has_prefix=has_prefix),
        grid=(NB // S, TT // R, nN),
        in_specs=in_specs,
        out_specs=[pl.BlockSpec((S, R, TN), lambda b, t, n: (b, t, n)),
                   pl.BlockSpec((S, pad, TN), lambda b, t, n: (b, t, n))],
        out_shape=[jax.ShapeDtypeStruct(x.shape, zdt),
                   jax.ShapeDtypeStruct((NB, (TT // R) * pad, D), F32)],
        scratch_shapes=scratch,
        compiler_params=_cparams(("arbitrary", "arbitrary", "arbitrary"), 40),
        name=mode,
    )(*args)
    return z, tails[:, -(width - 1):]


def _rope_lanes(x, cos, sin, period, half):
    lane = lax.broadcasted_iota(jnp.int32, x.shape, x.ndim - 1) % period
    width = x.shape[-1]
    partner = jnp.where(lane < half, pltpu.roll(x, width - half, x.ndim - 1),
                        pltpu.roll(x, half, x.ndim - 1))
    return x * cos + partner * sin


def _aproj_kernel(x_ref, sh_ref, sc_ref, g_ref, w_ref, wt_ref,
                  ch_ref, sh_h_ref, ci_ref, si_ref, ct_ref, st_ref,
                  q_ref, k32_ref, kb_ref, v32_ref, vb_ref, qi_ref, ki32_ref, kib_ref, wi_ref,
                  h_scr, *, S, R):
    n = pl.program_id(2)
    M = S * R
    TN = 4 * HEAD_DIM

    @pl.when(n == 0)
    def _():
        h = _rms_mod(x_ref[...], g_ref[...], sh_ref[...], sc_ref[...])
        h_scr[...] = h.reshape(M, D).astype(BF)

    def heads_rope(p):
        cos, sin = ch_ref[...], sh_h_ref[...]
        return jnp.concatenate(
            [_rope_lanes(p[:, i * HEAD_DIM:(i + 1) * HEAD_DIM], cos, sin, HEAD_DIM, ROPE_DIM // 2)
             for i in range(TN // HEAD_DIM)], axis=-1)

    @pl.when(n < 4)
    def _():
        p = jnp.dot(h_scr[...], w_ref[...], preferred_element_type=F32)
        q = heads_rope(p) * (HEAD_DIM ** -0.5)
        q_ref[...] = q.reshape(S, R, TN).astype(BF)

    @pl.when(n == 4)
    def _():
        p = jnp.dot(h_scr[...], w_ref[...], preferred_element_type=F32)
        k = heads_rope(p).reshape(S, R, TN)
        k32_ref[...] = k
        kb_ref[...] = k.astype(BF)

    @pl.when(n == 5)
    def _():
        v = jnp.dot(h_scr[...], w_ref[...], preferred_element_type=F32).reshape(S, R, TN)
        v32_ref[...] = v
        vb_ref[...] = v.astype(BF)

    @pl.when((n == 6) | (n == 7))
    def _():
        p = jnp.dot(h_scr[...], w_ref[...], preferred_element_type=F32)
        cos, sin = ci_ref[...], si_ref[...]
        qi = jnp.concatenate(
            [_rope_lanes(p[:, i * LANE:(i + 1) * LANE], cos, sin, IDX_DIM, IDX_ROPE_DIM // 2)
             for i in range(TN // LANE)], axis=-1)
        qi_ref[...] = qi.reshape(S, R, TN).astype(BF)

    @pl.when(n == 8)
    def _():
        p = jnp.dot(h_scr[...], wt_ref[...], preferred_element_type=F32)
        tl = _rope_lanes(p, ct_ref[...], st_ref[...], LANE, IDX_ROPE_DIM // 2)
        ki = tl[:, 0:IDX_DIM].reshape(S, R, IDX_DIM)
        ki32_ref[...] = ki
        kib_ref[...] = ki.astype(BF)
        wi_ref[...] = (tl[:, IDX_DIM:IDX_DIM + IDX_HEADS] * IDX_SCALE).reshape(S, R, IDX_HEADS)


def _aproj_call(x, mod, g, w_main, w_tail, tabs, S, R):
    NB, TT, _ = x.shape
    M = S * R
    TN = 4 * HEAD_DIM
    tab_spec = pl.BlockSpec((M, LANE), lambda b, t, n: (t, 0))

    def out(width, dt, idx=lambda b, t, n: (b, t, 0)):
        return (pl.BlockSpec((S, R, min(width, TN)), idx), jax.ShapeDtypeStruct((NB, TT, width), dt))

    outs = [out(Q_W, BF, lambda b, t, n: (b, t, jnp.minimum(n, 3))),
            out(KV_W, F32), out(KV_W, BF), out(KV_W, F32), out(KV_W, BF),
            out(IDXQ_W, BF, lambda b, t, n: (b, t, jnp.clip(n - 6, 0, 1))),
            out(IDX_DIM, F32), out(IDX_DIM, BF), out(IDX_HEADS, F32)]
    return pl.pallas_call(
        functools.partial(_aproj_kernel, S=S, R=R),
        grid=(NB // S, TT // R, 9),
        in_specs=[_row_spec(S, R, D), _mod_spec(S, 3), _mod_spec(S, 4), _const_spec((1, D)),
                  pl.BlockSpec((D, TN), lambda b, t, n: (0, jnp.minimum(n, ATTN_MAIN // TN - 1))),
                  _const_spec((D, LANE))] + [tab_spec] * 6,
        out_specs=[o[0] for o in outs],
        out_shape=[o[1] for o in outs],
        scratch_shapes=[pltpu.VMEM((M, D), BF)],
        compiler_params=_cparams(("arbitrary", "arbitrary", "arbitrary"), 40),
        name="attn_proj",
    )(x, mod, mod, g, w_main, w_tail, *tabs)


def _rope_tables(pos, reps):
    def tab(rot_dim, period, lanes_used):
        half = rot_dim // 2
        inv = jnp.exp(-math.log(ROPE_THETA) * jnp.arange(half, dtype=F32) * (2.0 / rot_dim))
        ang = pos.astype(F32)[:, None] * inv[None, :]
        cos, sin = jnp.cos(ang), jnp.sin(ang)
        n = pos.shape[0]
        one = jnp.ones((n, period - rot_dim), F32)
        zero = jnp.zeros((n, period - rot_dim), F32)
        c = jnp.concatenate([cos, cos, one], axis=1)
        s = jnp.concatenate([-sin, sin, zero], axis=1)
        c = jnp.tile(c, (1, lanes_used // period))
        s = jnp.tile(s, (1, lanes_used // period))
        if lanes_used < LANE:
            c = jnp.concatenate([c, jnp.ones((n, LANE - lanes_used), F32)], axis=1)
            s = jnp.concatenate([s, jnp.zeros((n, LANE - lanes_used), F32)], axis=1)
        return [jnp.tile(c, (reps, 1)), jnp.tile(s, (reps, 1))]

    return tab(ROPE_DIM, HEAD_DIM, LANE) + tab(IDX_ROPE_DIM, IDX_DIM, LANE) + tab(IDX_ROPE_DIM, IDX_DIM, IDX_DIM)


def _acore_kernel(q_ref, qi_ref, wi_ref, ki_ref, kb_ref, vb_ref, o_ref,
                  keys, m_scr, l_scr, acc_scr, *, TQ, KT, NT, pos0, l_valid, n_top):
    j = pl.program_id(1)
    q0 = pos0 + j * TQ
    last_key = ((q0 + TQ - 1) // CHUNK) * CHUNK + CHUNK - 1
    nt = jnp.minimum(last_key // KT + 1, NT)
    qchunk = (q0 + lax.broadcasted_iota(jnp.int32, (TQ, 1), 0)) // CHUNK

    qi = qi_ref[0]
    qst = jnp.concatenate([qi[:, h * IDX_DIM:(h + 1) * IDX_DIM] for h in range(IDX_HEADS)], axis=0)
    wi = wi_ref[0]
    wst = jnp.concatenate([wi[:, h:h + 1] for h in range(IDX_HEADS)], axis=0)

    def score_body(kt, carry):
        start = pl.multiple_of(kt * KT, KT)
        kit = ki_ref[0, pl.ds(start, KT), :]
        s = lax.dot_general(qst, kit, (((1,), (1,)), ((), ())), preferred_element_type=F32)
        s = jnp.maximum(s, 0.0) * wst
        sc = jnp.sum(s.reshape(IDX_HEADS, TQ, KT), axis=0) + 0.0
        bits = lax.bitcast_convert_type(sc, jnp.int32)
        key = jnp.where(bits < 0, bits ^ jnp.int32(0x7FFFFFFF), bits)
        kpos = start + lax.broadcasted_iota(jnp.int32, (1, KT), 1)
        adm = (kpos // CHUNK <= qchunk) & (kpos < l_valid)
        keys[kt] = jnp.where(adm, key, jnp.int32(INT_MIN))
        return carry

    lax.fori_loop(0, nt, score_body, 0)

    def count_ge(cand):
        def body(kt, acc):
            ge = jnp.where(keys[kt] >= cand, 1.0, 0.0)
            part = ge[:, 0:LANE]
            for c in range(1, KT // LANE):
                part = part + ge[:, c * LANE:(c + 1) * LANE]
            return acc + part
        acc = lax.fori_loop(0, nt, body, jnp.zeros((TQ, LANE), F32))
        return jnp.sum(acc, axis=-1, keepdims=True)

    def bit_body(i, thr):
        bit = 31 - i
        cand = jnp.where(i == 0, jnp.zeros_like(thr), thr | jnp.left_shift(jnp.int32(1), bit))
        return jnp.where(count_ge(cand) >= float(n_top), cand, thr)

    thr = lax.fori_loop(0, 32, bit_body, jnp.full((TQ, 1), INT_MIN, jnp.int32))
    thr = jnp.maximum(thr, jnp.int32(INT_MIN + 1))

    q = q_ref[0]
    G = HEAD_GROUP
    for g in range(KV_HEADS):
        qg = jnp.concatenate(
            [q[:, (G * g + r) * HEAD_DIM:(G * g + r + 1) * HEAD_DIM] for r in range(G)], axis=0)
        m_scr[...] = jnp.full((G * TQ, 1), NEG_BIG, F32)
        l_scr[...] = jnp.zeros((G * TQ, 1), F32)
        acc_scr[...] = jnp.zeros((G * TQ, HEAD_DIM), F32)

        def att_body(kt, carry, g=g, qg=qg):
            start = pl.multiple_of(kt * KT, KT)
            kt_ = kb_ref[0, pl.ds(start, KT), g * HEAD_DIM:(g + 1) * HEAD_DIM]
            vt_ = vb_ref[0, pl.ds(start, KT), g * HEAD_DIM:(g + 1) * HEAD_DIM]
            s = lax.dot_general(qg, kt_, (((1,), (1,)), ((), ())), preferred_element_type=F32)
            sel = keys[kt] >= thr
            s = jnp.where(sel[None], s.reshape(G, TQ, KT), NEG_BIG).reshape(G * TQ, KT)
            m_old = m_scr[...]
            m_new = jnp.maximum(m_old, jnp.max(s, axis=-1, keepdims=True))
            alpha = jnp.exp(m_old - m_new)
            p = jnp.exp(s - m_new)
            l_scr[...] = alpha * l_scr[...] + jnp.sum(p, axis=-1, keepdims=True)
            acc_scr[...] = alpha * acc_scr[...] + jnp.dot(p.astype(BF), vt_, preferred_element_type=F32)
            m_scr[...] = m_new
            return carry

        lax.fori_loop(0, nt, att_body, 0)
        og = acc_scr[...] / l_scr[...]
        for r in range(G):
            o_ref[0, :, (G * g + r) * HEAD_DIM:(G * g + r + 1) * HEAD_DIM] = og[r * TQ:(r + 1) * TQ].astype(BF)


def _acore_call(q, qi, wi, kib, kb, vb, TQ, KT, pos0, l_valid):
    NB, TT, _ = q.shape
    LP = kb.shape[1]
    NT = LP // KT
    n_top = min(TOPK, l_valid // 4)
    res = lambda w: pl.BlockSpec((1, LP, w), lambda b, j: (b, 0, 0), pipeline_mode=pl.Buffered(1))
    blk = lambda w: pl.BlockSpec((1, TQ, w), lambda b, j: (b, j, 0))
    G = HEAD_GROUP
    return pl.pallas_call(
        functools.partial(_acore_kernel, TQ=TQ, KT=KT, NT=NT, pos0=pos0, l_valid=l_valid, n_top=n_top),
        grid=(NB, TT // TQ),
        in_specs=[blk(Q_W), blk(IDXQ_W), blk(IDX_HEADS), res(IDX_DIM), res(KV_W), res(KV_W)],
        out_specs=blk(Q_W),
        out_shape=jax.ShapeDtypeStruct((NB, TT, Q_W), BF),
        scratch_shapes=[pltpu.VMEM((NT, TQ, KT), jnp.int32), pltpu.VMEM((G * TQ, 1), F32),
                        pltpu.VMEM((G * TQ, 1), F32), pltpu.VMEM((G * TQ, HEAD_DIM), F32)],
        compiler_params=_cparams(("arbitrary", "arbitrary"), 56),
        name="attn_core",
    )(q, qi, wi, kib, kb, vb)


def _trunk(x, mods, S, R, TF, TN, W, pos0, past):
    NB, TT, _ = x.shape
    fresh = past is None
    states = {}
    for i in range(DEPTH):
        kind = i % 4
        mod = mods[i]
        gn = W["g_norm"][i]
        x = _ffn_call(x, mod, gn[0:1], W["wg"], W["wu"], W["wd"], i, 0, 0, S, R, TF)
        if kind == 0:
            prefix = None if fresh else jnp.pad(past["pool"], ((0, 0), (1, 0), (0, 0)))
            x, st = _pool_call(x, mod, gn[1:2], W["w_pool"], W["s_pool"], R if fresh else TT, prefix)
            states["pool"] = st[:, 1:]
        elif kind == 1:
            prefix = None if fresh else jnp.pad(past["sconv"], ((0, 0), (SCONV_PAD - 2, 0), (0, 0)))
            z, st = _conv_call(x, mod, gn[1:2], W["w_sc_in"], "sconv", S, R, TN, W["w_sc_conv"],
                               prefix=prefix)
            states["sconv"] = st
            x = _linres_call(z, x, mod, 5, W["w_sc_out"], S, R, TN)
        elif kind == 2:
            reps = 1 if fresh else NB
            tabs = _rope_tables(pos0 + jnp.arange(TT), reps)
            q, k32, kb, v32, vb, qi, ki32, kib, wi = _aproj_call(
                x, mod, gn[1:2], W["w_attn_main"], W["w_attn_tail"], tabs, S, R)
            states["k"], states["v"], states["kidx"] = k32, v32, ki32
            if fresh:
                o = _acore_call(q, qi, wi, kib, kb, vb, 128, 512, pos0, TT)
            else:
                KT = 512
                L = past["k"].shape[1] + TT
                LP = -(-L // KT) * KT

                def cat(old, new):
                    return jnp.pad(jnp.concatenate([old.astype(BF), new], axis=1),
                                   ((0, 0), (0, LP - L), (0, 0)))

                o = _acore_call(q, qi, wi, cat(past["kidx"], kib), cat(past["k"], kb),
                                cat(past["v"], vb), TT, KT, pos0, L)
            x = _linres_call(o, x, mod, 5, W["w_attn_out"], S, R, TN)
        else:
            prefix = None if fresh else jnp.pad(past["cconv"], ((0, 0), (CCONV_PAD - 30, 0), (0, 0)))
            cv, st = _conv_call(x, mod, gn[1:2], W["w_cm_pw1"], "cconv", S, R, TN, W["w_cm_dw"],
                                pbias=W["b_cm_pw1"], b_dw=W["b_cm_dw"], prefix=prefix)
            states["cconv"] = st
            x = _linres_call(cv, x, mod, 5, W["w_cm_pw2"], S, R, TN,
                             ln=(W["g_cm_ln"], W["b_cm_ln"]), bias=W["b_cm_pw2"])
        gfin = W["g_final"] if i == DEPTH - 1 else None
        x = _ffn_call(x, mod, gn[2:3], W["wg"], W["wu"], W["wd"], i, 1, 2, S, R, TF, gfin)
    return x, states


def kernel(x_prompt, x_sample, state_pool, state_sconv, cache_k, cache_v, cache_kidx, state_cconv, c_prompt, c_sample, w_mod, b_mod, g_norm, w_ffn_gate, w_ffn_up, w_ffn_down, w_pool, s_pool, w_sc_in, w_sc_conv, w_sc_out, w_attn_in, w_attn_out, w_cm_pw1, b_cm_pw1, w_cm_dw, b_cm_dw, g_cm_ln, b_cm_ln, w_cm_pw2, b_cm_pw2, g_final):
    B, T, _ = x_prompt.shape
    NS, TS, _ = x_sample.shape
    assert D_FF % 512 == 0 and T % 512 == 0 and B + NS <= MOD_ROWS
    assert w_pool.shape[0] == w_sc_in.shape[0] == w_attn_in.shape[0] == w_cm_pw1.shape[0] == 1

    c_all = jnp.concatenate([c_prompt, c_sample, jnp.zeros((MOD_ROWS - B - NS, D), F32)], axis=0)
    mod_all = _mod_call(c_all, w_mod, b_mod)
    mods_p = [mod_all[i, 0:B].reshape(B, 1, 9 * D) for i in range(DEPTH)]
    mods_s = [mod_all[i, B:B + NS].reshape(NS, 1, 9 * D) for i in range(DEPTH)]

    wa = w_attn_in[0].astype(BF)
    W = dict(
        g_norm=g_norm, g_final=g_final.reshape(1, D),
        wg=w_ffn_gate.astype(BF), wu=w_ffn_up.astype(BF), wd=w_ffn_down.astype(BF),
        w_pool=w_pool[0].astype(BF), s_pool=s_pool,
        w_sc_in=w_sc_in[0].astype(BF), w_sc_conv=w_sc_conv[0], w_sc_out=w_sc_out[0].astype(BF),
        w_attn_main=wa, w_attn_tail=jnp.pad(wa[:, ATTN_MAIN:], ((0, 0), (0, LANE - ATTN_TAIL))),
        w_attn_out=w_attn_out[0].astype(BF),
        w_cm_pw1=w_cm_pw1[0].astype(BF), b_cm_pw1=b_cm_pw1, w_cm_dw=w_cm_dw[0], b_cm_dw=b_cm_dw,
        g_cm_ln=g_cm_ln, b_cm_ln=b_cm_ln, w_cm_pw2=w_cm_pw2[0].astype(BF), b_cm_pw2=b_cm_pw2,
    )

    y_p, st_p = _trunk(x_prompt, mods_p, 1, 512, 512, 512, W, 0, None)
    past = dict(pool=state_pool[0], sconv=state_sconv[0], cconv=state_cconv[0],
                k=cache_k[0].reshape(NS, -1, KV_W), v=cache_v[0].reshape(NS, -1, KV_W),
                kidx=cache_kidx[0])
    y_s, st_s = _trunk(x_sample, mods_s, NS, TS, 512, 512, W, cache_k.shape[2], past)

    def kv(a):
        return a.reshape(1, a.shape[0], a.shape[1], KV_HEADS, HEAD_DIM)

    return (y_p, y_s, st_p["pool"][None], st_s["pool"][None], st_p["sconv"][None], st_s["sconv"][None],
            kv(st_p["k"]), kv(st_s["k"]), kv(st_p["v"]), kv(st_s["v"]),
            st_p["kidx"][None], st_s["kidx"][None], st_p["cconv"][None], st_s["cconv"][None])
```

```python
import functools
import math

import jax
import jax.numpy as jnp
import numpy as np
from jax import lax
from jax.experimental import pallas as pl
from jax.experimental.pallas import tpu as pltpu

D = 2048
D_FF = 5632
DEPTH = 4
NORM_EPS = 1e-6
CHUNK = 64
POOL_WINDOWS = (2, 4, 8, 16)
POOL_GROUP = D // 4
POOL_PAD = 16
SCONV_WIDTH = 3
SCONV_PAD = 8
CCONV_WIDTH = 31
CCONV_PAD = 32
N_HEADS = 16
HEAD_DIM = 128
KV_HEADS = 4
HEAD_GROUP = N_HEADS // KV_HEADS
ROPE_DIM = HEAD_DIM // 4
ROPE_THETA = 500000.0
IDX_HEADS = 16
IDX_DIM = 64
IDX_ROPE_DIM = IDX_DIM // 4
IDX_SCALE = IDX_HEADS ** -0.5 * IDX_DIM ** -0.5
TOPK = 256
Q_W = N_HEADS * HEAD_DIM
KV_W = KV_HEADS * HEAD_DIM
IDXQ_W = IDX_HEADS * IDX_DIM
ATTN_MAIN = Q_W + 2 * KV_W + IDXQ_W
ATTN_TAIL = IDX_DIM + IDX_HEADS
LANE = 128
MOD_ROWS = 40
Q_SCALE = HEAD_DIM ** -0.5 * math.log2(math.e)
INT_MIN = -2 ** 31
NEG_BIG = -1e30

BF = jnp.bfloat16
F32 = jnp.float32


def _cparams(sem, vmem_mib):
    return pltpu.CompilerParams(dimension_semantics=sem, vmem_limit_bytes=vmem_mib << 20)


def _rms(x, g):
    return x * lax.rsqrt(jnp.mean(x * x, axis=-1, keepdims=True) + NORM_EPS) * g


def _rms_mod(x, g, shift, scale):
    return _rms(x, g) * (1.0 + scale) + shift


def _silu(x):
    return x * jax.nn.sigmoid(x)


def _row_spec(S, R, W):
    return pl.BlockSpec((S, R, W), lambda b, t, n: (b, t, 0))


def _mod_spec(S, col):
    return pl.BlockSpec((S, 1, D), lambda b, t, n: (b, 0, col))


def _const_spec(shape):
    nd = len(shape)
    return pl.BlockSpec(shape, lambda b, t, n: (0,) * nd)


def _mod_kernel(c_ref, w_ref, b_ref, o_ref):
    a = _silu(c_ref[...]).astype(BF)
    o_ref[0] = jnp.dot(a, w_ref[0].astype(BF), preferred_element_type=F32) + b_ref[0]


def _mod_call(c_all, w_mod, b_mod):
    TN = 1024
    N = 9 * D
    return pl.pallas_call(
        _mod_kernel,
        grid=(DEPTH, N // TN),
        in_specs=[pl.BlockSpec((MOD_ROWS, D), lambda i, n: (0, 0)),
                  pl.BlockSpec((1, D, TN), lambda i, n: (i, 0, n)),
                  pl.BlockSpec((1, 1, TN), lambda i, n: (i, 0, n))],
        out_specs=pl.BlockSpec((1, MOD_ROWS, TN), lambda i, n: (i, 0, n)),
        out_shape=jax.ShapeDtypeStruct((DEPTH, MOD_ROWS, N), F32),
        compiler_params=_cparams(("arbitrary", "arbitrary"), 40),
        name="mod",
    )(c_all, w_mod, b_mod.reshape(DEPTH, 1, N))


def _ffn_kernel(*refs, S, R, final):
    if final:
        x_ref, sh_ref, sc_ref, gt_ref, g_ref, wg_ref, wu_ref, wd_ref, gf_ref, o_ref, h_scr = refs
    else:
        x_ref, sh_ref, sc_ref, gt_ref, g_ref, wg_ref, wu_ref, wd_ref, o_ref, h_scr = refs
    f = pl.program_id(2)
    M = S * R

    @pl.when(f == 0)
    def _():
        h = _rms_mod(x_ref[...], g_ref[...], sh_ref[...], sc_ref[...])
        h_scr[...] = h.reshape(M, D).astype(BF)
        o_ref[...] = jnp.zeros_like(o_ref)

    h = h_scr[...]
    g = jnp.dot(h, wg_ref[...], preferred_element_type=F32)
    u = jnp.dot(h, wu_ref[...], preferred_element_type=F32)
    a = (_silu(g) * u).astype(BF)
    o_ref[...] += jnp.dot(a, wd_ref[...], preferred_element_type=F32).reshape(S, R, D)

    @pl.when(f == pl.num_programs(2) - 1)
    def _():
        y = x_ref[...] + 0.5 * gt_ref[...] * o_ref[...]
        if final:
            y = _rms(y, gf_ref[...])
        o_ref[...] = y


def _ffn_call(x, mod, g, wg, wu, wd, layer, which, sub, S, R, TF, gfin=None):
    NB, TT, _ = x.shape
    final = gfin is not None
    in_specs = [_row_spec(S, R, D), _mod_spec(S, 3 * sub), _mod_spec(S, 3 * sub + 1),
                _mod_spec(S, 3 * sub + 2), _const_spec((1, D)),
                pl.BlockSpec((None, None, D, TF), lambda b, t, f: (layer, which, 0, f)),
                pl.BlockSpec((None, None, D, TF), lambda b, t, f: (layer, which, 0, f)),
                pl.BlockSpec((None, None, TF, D), lambda b, t, f: (layer, which, f, 0))]
    args = [x, mod, mod, mod, g, wg, wu, wd]
    if final:
        in_specs.append(_const_spec((1, D)))
        args.append(gfin)
    return pl.pallas_call(
        functools.partial(_ffn_kernel, S=S, R=R, final=final),
        grid=(NB // S, TT // R, D_FF // TF),
        in_specs=in_specs,
        out_specs=_row_spec(S, R, D),
        out_shape=jax.ShapeDtypeStruct(x.shape, F32),
        scratch_shapes=[pltpu.VMEM((S * R, D), BF)],
        compiler_params=_cparams(("arbitrary", "arbitrary", "arbitrary"), 48),
        name="ffn",
    )(*args)


def _linres_kernel(*refs, S, R, ln, bias):
    refs = list(refs)
    a_ref, x_ref, gt_ref, w_ref = refs[:4]
    rest = refs[4:]
    if ln:
        gl_ref, bl_ref = rest[:2]
        rest = rest[2:]
    if bias:
        b_ref = rest[0]
        rest = rest[1:]
    o_ref, a_scr = rest
    n = pl.program_id(2)
    M = S * R

    @pl.when(n == 0)
    def _():
        a = a_ref[...]
        if ln:
            a = a.astype(F32)
            mu = jnp.mean(a, axis=-1, keepdims=True)
            ac = a - mu
            a = ac * lax.rsqrt(jnp.mean(ac * ac, axis=-1, keepdims=True) + NORM_EPS)
            a = _silu(a * gl_ref[...] + bl_ref[...])
        a_scr[...] = a.reshape(M, a.shape[-1]).astype(BF)

    y = jnp.dot(a_scr[...], w_ref[...], preferred_element_type=F32)
    if bias:
        y = y + b_ref[...]
    o_ref[...] = x_ref[...] + gt_ref[...] * y.reshape(o_ref.shape)


def _linres_call(a, x, mod, gate_col, w, S, R, TN, ln=None, bias=None):
    NB, TT, K = a.shape
    in_specs = [_row_spec(S, R, K),
                pl.BlockSpec((S, R, TN), lambda b, t, n: (b, t, n)),
                pl.BlockSpec((S, 1, TN), lambda b, t, n: (b, 0, gate_col * (D // TN) + n)),
                pl.BlockSpec((K, TN), lambda b, t, n: (0, n))]
    args = [a, x, mod, w]
    if ln is not None:
        in_specs += [_const_spec((1, K)), _const_spec((1, K))]
        args += list(ln)
    if bias is not None:
        in_specs.append(pl.BlockSpec((1, TN), lambda b, t, n: (0, n)))
        args.append(bias)
    return pl.pallas_call(
        functools.partial(_linres_kernel, S=S, R=R, ln=ln is not None, bias=bias is not None),
        grid=(NB // S, TT // R, D // TN),
        in_specs=in_specs,
        out_specs=pl.BlockSpec((S, R, TN), lambda b, t, n: (b, t, n)),
        out_shape=jax.ShapeDtypeStruct(x.shape, F32),
        scratch_shapes=[pltpu.VMEM((S * R, K), BF)],
        compiler_params=_cparams(("arbitrary", "arbitrary", "arbitrary"), 40),
        name="linres",
    )(*args)


def _pool_kernel(*refs, R, has_prefix, n_valid_prefix):
    if has_prefix:
        (x_ref, sh_ref, sc_ref, gt_ref, g_ref, wp_ref, sp_ref, pre_ref,
         o_ref, st_ref, hext) = refs
    else:
        (x_ref, sh_ref, sc_ref, gt_ref, g_ref, wp_ref, sp_ref,
         o_ref, st_ref, hext, carry) = refs
    t = pl.program_id(1)
    x = x_ref[0]
    h = _rms_mod(x, g_ref[...], sh_ref[0], sc_ref[0])
    if has_prefix:
        hext[0:POOL_PAD, :] = pre_ref[0]
    else:
        @pl.when(t == 0)
        def _():
            hext[0:POOL_PAD, :] = jnp.zeros((POOL_PAD, D), F32)

        @pl.when(t > 0)
        def _():
            hext[0:POOL_PAD, :] = carry[...]

        carry[...] = h[R - POOL_PAD:R, :]
    hext[POOL_PAD:POOL_PAD + R, :] = h
    st_ref[0] = h[R - POOL_PAD:R, :]
    tpos = t * R + lax.broadcasted_iota(jnp.int32, (R, 1), 0)
    gate = gt_ref[0]
    for gi, w in enumerate(POOL_WINDOWS):
        lo = gi * POOL_GROUP
        hi = lo + POOL_GROUP
        win = h[:, lo:hi]
        for j in range(1, w):
            win = win + hext[POOL_PAD - j:POOL_PAD - j + R, lo:hi]
        cnt = jnp.minimum(tpos + 1 + n_valid_prefix, w).astype(F32)
        pooled = (win / cnt - h[:, lo:hi]).astype(BF)
        y = jnp.dot(pooled, wp_ref[gi], preferred_element_type=F32) * sp_ref[:, lo:hi]
        o_ref[0, :, lo:hi] = x[:, lo:hi] + gate[:, lo:hi] * y


def _pool_call(x, mod, g, w_pool, s_pool, R, prefix=None):
    NB, TT, _ = x.shape
    has_prefix = prefix is not None
    in_specs = [_row_spec(1, R, D), _mod_spec(1, 3), _mod_spec(1, 4), _mod_spec(1, 5),
                _const_spec((1, D)), _const_spec((4, POOL_GROUP, POOL_GROUP)), _const_spec((1, D))]
    args = [x, mod, mod, mod, g, w_pool, s_pool]
    scratch = [pltpu.VMEM((R + POOL_PAD, D), F32)]
    if has_prefix:
        in_specs.append(pl.BlockSpec((1, POOL_PAD, D), lambda b, t, n: (b, 0, 0)))
        args.append(prefix)
    else:
        scratch.append(pltpu.VMEM((POOL_PAD, D), F32))
    return pl.pallas_call(
        functools.partial(_pool_kernel, R=R, has_prefix=has_prefix,
                          n_valid_prefix=POOL_PAD - 1 if has_prefix else 0),
        grid=(NB, TT // R, 1),
        in_specs=in_specs,
        out_specs=[_row_spec(1, R, D), pl.BlockSpec((1, POOL_PAD, D), lambda b, t, n: (b, 0, 0))],
        out_shape=[jax.ShapeDtypeStruct(x.shape, F32), jax.ShapeDtypeStruct((NB, POOL_PAD, D), F32)],
        scratch_shapes=scratch,
        compiler_params=_cparams(("arbitrary", "arbitrary", "arbitrary"), 40),
        name="pool",
    )(*args)


def _conv_kernel(*refs, S, R, TN, mode, has_prefix):
    refs = list(refs)
    x_ref, sh_ref, sc_ref, g_ref = refs[:4]
    rest = refs[4:]
    nproj = 3 if mode == "sconv" else 2
    w_refs = rest[:nproj]
    rest = rest[nproj:]
    if mode == "cconv":
        pb_refs = rest[:2]
        wc_ref, bdw_ref = rest[2:4]
        rest = rest[4:]
        width, pad = CCONV_WIDTH, CCONV_PAD
    else:
        wc_ref = rest[0]
        rest = rest[1:]
        width, pad = SCONV_WIDTH, SCONV_PAD
    if has_prefix:
        pre_ref = rest[0]
        z_ref, st_ref, h_scr, uext = rest[1:]
    else:
        z_ref, st_ref, h_scr, uext, carry = rest
    t = pl.program_id(1)
    n = pl.program_id(2)
    M = S * R

    @pl.when(n == 0)
    def _():
        h = _rms_mod(x_ref[...], g_ref[...], sh_ref[...], sc_ref[...])
        h_scr[...] = h.reshape(M, D).astype(BF)

    h = h_scr[...]
    proj = [jnp.dot(h, w_ref[...], preferred_element_type=F32) for w_ref in w_refs]
    if mode == "sconv":
        bq, cq, vq = proj
        u = cq * vq
    else:
        a = proj[0] + pb_refs[0][...]
        gq = proj[1] + pb_refs[1][...]
        u = a * jax.nn.sigmoid(gq)
    u3 = u.reshape(S, R, TN)
    if has_prefix:
        uext[:, 0:pad, :] = pre_ref[...]
    else:
        @pl.when(t == 0)
        def _():
            uext[:, 0:pad, :] = jnp.zeros((S, pad, TN), F32)

        @pl.when(t > 0)
        def _():
            uext[:, 0:pad, :] = carry[n]
    uext[:, pad:pad + R, :] = u3
    wc = wc_ref[...]
    conv = wc[width - 1:width, :] * u3
    for j in range(width - 1):
        off = pad - (width - 1) + j
        conv = conv + wc[j:j + 1, :] * uext[:, off:off + R, :]
    tail = uext[:, R:R + pad, :]
    if not has_prefix:
        carry[n] = tail
    st_ref[...] = tail
    if mode == "sconv":
        z_ref[...] = (bq.reshape(S, R, TN) * conv).astype(z_ref.dtype)
    else:
        z_ref[...] = conv + bdw_ref[...]


def _conv_call(x, mod, g, w, mode, S, R, TN, w_conv, pbias=None, b_dw=None, prefix=None):
    NB, TT, _ = x.shape
    has_prefix = prefix is not None
    nproj = 3 if mode == "sconv" else 2
    width, pad = (SCONV_WIDTH, SCONV_PAD) if mode == "sconv" else (CCONV_WIDTH, CCONV_PAD)
    nN = D // TN
    in_specs = [_row_spec(S, R, D), _mod_spec(S, 3), _mod_spec(S, 4), _const_spec((1, D))]
    args = [x, mod, mod, g]
    for k in range(nproj):
        in_specs.append(pl.BlockSpec((D, TN), lambda b, t, n, k=k: (0, k * nN + n)))
        args.append(w)
    if mode == "cconv":
        for k in range(2):
            in_specs.append(pl.BlockSpec((1, TN), lambda b, t, n, k=k: (0, k * nN + n)))
            args.append(pbias)
    in_specs.append(pl.BlockSpec((width, TN), lambda b, t, n: (0, n)))
    args.append(w_conv)
    if mode == "cconv":
        in_specs.append(pl.BlockSpec((1, TN), lambda b, t, n: (0, n)))
        args.append(b_dw)
    scratch = [pltpu.VMEM((S * R, D), BF), pltpu.VMEM((S, R + pad, TN), F32)]
    if has_prefix:
        in_specs.append(pl.BlockSpec((S, pad, TN), lambda b, t, n: (b, 0, n)))
        args.append(prefix)
    else:
        scratch.append(pltpu.VMEM((nN, S, pad, TN), F32))
    zdt = BF if mode == "sconv" else F32
    z, tails = pl.pallas_call(
        functools.partial(_conv_kernel, S=S, R=R, TN=TN, mode=mode, has_prefix=has_prefix),
        grid=(NB // S, TT // R, nN),
        in_specs=in_specs,
        out_specs=[pl.BlockSpec((S, R, TN), lambda b, t, n: (b, t, n)),
                   pl.BlockSpec((S, pad, TN), lambda b, t, n: (b, t, n))],
        out_shape=[jax.ShapeDtypeStruct(x.shape, zdt),
                   jax.ShapeDtypeStruct((NB, (TT // R) * pad, D), F32)],
        scratch_shapes=scratch,
        compiler_params=_cparams(("arbitrary", "arbitrary", "arbitrary"), 40),
        name=mode,
    )(*args)
    return z, tails[:, -(width - 1):]


def _rope_lanes(x, cos, sin, period, half):
    lane = lax.broadcasted_iota(jnp.int32, x.shape, x.ndim - 1) % period
    width = x.shape[-1]
    partner = jnp.where(lane < half, pltpu.roll(x, width - half, x.ndim - 1),
                        pltpu.roll(x, half, x.ndim - 1))
    return x * cos + partner * sin


def _aproj_kernel(x_ref, sh_ref, sc_ref, g_ref, w_ref, wt_ref,
                  ch_ref, sh_h_ref, ci_ref, si_ref, ct_ref, st_ref,
                  q_ref, k32_ref, kb_ref, v32_ref, vb_ref, qi_ref, ki32_ref, kib_ref, wi_ref,
                  h_scr, *, S, R):
    n = pl.program_id(2)
    M = S * R
    TN = 4 * HEAD_DIM

    @pl.when(n == 0)
    def _():
        h = _rms_mod(x_ref[...], g_ref[...], sh_ref[...], sc_ref[...])
        h_scr[...] = h.reshape(M, D).astype(BF)

    def heads_rope(p):
        cos, sin = ch_ref[...], sh_h_ref[...]
        return jnp.concatenate(
            [_rope_lanes(p[:, i * HEAD_DIM:(i + 1) * HEAD_DIM], cos, sin, HEAD_DIM, ROPE_DIM // 2)
             for i in range(TN // HEAD_DIM)], axis=-1)

    @pl.when(n < 4)
    def _():
        p = jnp.dot(h_scr[...], w_ref[...], preferred_element_type=F32)
        q = heads_rope(p) * Q_SCALE
        q_ref[...] = q.reshape(S, R, TN).astype(BF)

    @pl.when(n == 4)
    def _():
        p = jnp.dot(h_scr[...], w_ref[...], preferred_element_type=F32)
        k = heads_rope(p).reshape(S, R, TN)
        k32_ref[...] = k
        kb_ref[...] = k.astype(BF)

    @pl.when(n == 5)
    def _():
        v = jnp.dot(h_scr[...], w_ref[...], preferred_element_type=F32).reshape(S, R, TN)
        v32_ref[...] = v
        vb_ref[...] = v.astype(BF)

    @pl.when((n == 6) | (n == 7))
    def _():
        p = jnp.dot(h_scr[...], w_ref[...], preferred_element_type=F32)
        cos, sin = ci_ref[...], si_ref[...]
        qi = jnp.concatenate(
            [_rope_lanes(p[:, i * LANE:(i + 1) * LANE], cos, sin, IDX_DIM, IDX_ROPE_DIM // 2)
             for i in range(TN // LANE)], axis=-1)
        qi_ref[...] = qi.reshape(S, R, TN).astype(BF)

    @pl.when(n == 8)
    def _():
        p = jnp.dot(h_scr[...], wt_ref[...], preferred_element_type=F32)
        tl = _rope_lanes(p, ct_ref[...], st_ref[...], LANE, IDX_ROPE_DIM // 2)
        ki = tl[:, 0:IDX_DIM].reshape(S, R, IDX_DIM)
        ki32_ref[...] = ki
        kib_ref[...] = ki.astype(BF)
        wi_ref[...] = (tl[:, IDX_DIM:IDX_DIM + IDX_HEADS] * IDX_SCALE).reshape(S, R, IDX_HEADS)


def _aproj_call(x, mod, g, w_main, w_tail, tabs, S, R):
    NB, TT, _ = x.shape
    M = S * R
    TN = 4 * HEAD_DIM
    tab_spec = pl.BlockSpec((M, LANE), lambda b, t, n: (t, 0))

    def out(width, dt, idx=lambda b, t, n: (b, t, 0)):
        return (pl.BlockSpec((S, R, min(width, TN)), idx), jax.ShapeDtypeStruct((NB, TT, width), dt))

    outs = [out(Q_W, BF, lambda b, t, n: (b, t, jnp.minimum(n, 3))),
            out(KV_W, F32), out(KV_W, BF), out(KV_W, F32), out(KV_W, BF),
            out(IDXQ_W, BF, lambda b, t, n: (b, t, jnp.clip(n - 6, 0, 1))),
            out(IDX_DIM, F32), out(IDX_DIM, BF), out(IDX_HEADS, F32)]
    return pl.pallas_call(
        functools.partial(_aproj_kernel, S=S, R=R),
        grid=(NB // S, TT // R, 9),
        in_specs=[_row_spec(S, R, D), _mod_spec(S, 3), _mod_spec(S, 4), _const_spec((1, D)),
                  pl.BlockSpec((D, TN), lambda b, t, n: (0, jnp.minimum(n, ATTN_MAIN // TN - 1))),
                  _const_spec((D, LANE))] + [tab_spec] * 6,
        out_specs=[o[0] for o in outs],
        out_shape=[o[1] for o in outs],
        scratch_shapes=[pltpu.VMEM((M, D), BF)],
        compiler_params=_cparams(("arbitrary", "arbitrary", "arbitrary"), 40),
        name="attn_proj",
    )(x, mod, mod, g, w_main, w_tail, *tabs)


def _rope_tables(pos, reps):
    def tab(rot_dim, period, lanes_used):
        half = rot_dim // 2
        inv = jnp.exp(-math.log(ROPE_THETA) * jnp.arange(half, dtype=F32) * (2.0 / rot_dim))
        ang = pos.astype(F32)[:, None] * inv[None, :]
        cos, sin = jnp.cos(ang), jnp.sin(ang)
        n = pos.shape[0]
        one = jnp.ones((n, period - rot_dim), F32)
        zero = jnp.zeros((n, period - rot_dim), F32)
        c = jnp.concatenate([cos, cos, one], axis=1)
        s = jnp.concatenate([-sin, sin, zero], axis=1)
        c = jnp.tile(c, (1, lanes_used // period))
        s = jnp.tile(s, (1, lanes_used // period))
        if lanes_used < LANE:
            c = jnp.concatenate([c, jnp.ones((n, LANE - lanes_used), F32)], axis=1)
            s = jnp.concatenate([s, jnp.zeros((n, LANE - lanes_used), F32)], axis=1)
        return [jnp.tile(c, (reps, 1)), jnp.tile(s, (reps, 1))]

    return tab(ROPE_DIM, HEAD_DIM, LANE) + tab(IDX_ROPE_DIM, IDX_DIM, LANE) + tab(IDX_ROPE_DIM, IDX_DIM, IDX_DIM)


def _acore_kernel(q_ref, qi_ref, wi_ref, ki_ref, kb_ref, vb_ref, o_ref,
                  keys, m_scr, l_scr, acc_scr, *, TQ, KT, NT, pos0, l_valid, n_top):
    j = pl.program_id(1)
    q0 = pos0 + j * TQ
    last_key = ((q0 + TQ - 1) // CHUNK) * CHUNK + CHUNK - 1
    nt = jnp.minimum(last_key // KT + 1, NT)
    qchunk = (q0 + lax.broadcasted_iota(jnp.int32, (TQ, 1), 0)) // CHUNK

    qi = qi_ref[0]
    qst = jnp.concatenate([qi[:, h * IDX_DIM:(h + 1) * IDX_DIM] for h in range(IDX_HEADS)], axis=0)
    wi = wi_ref[0]
    wst = jnp.concatenate([wi[:, h:h + 1] for h in range(IDX_HEADS)], axis=0)

    def score_body(kt, carry):
        start = pl.multiple_of(kt * KT, KT)
        kit = ki_ref[0, pl.ds(start, KT), :]
        s = lax.dot_general(qst, kit, (((1,), (1,)), ((), ())), preferred_element_type=F32)
        s = jnp.maximum(s, 0.0) * wst
        sc = jnp.sum(s.reshape(IDX_HEADS, TQ, KT), axis=0) + 0.0
        bits = lax.bitcast_convert_type(sc, jnp.int32)
        key = jnp.where(bits < 0, bits ^ jnp.int32(0x7FFFFFFF), bits)
        kpos = start + lax.broadcasted_iota(jnp.int32, (1, KT), 1)
        adm = (kpos // CHUNK <= qchunk) & (kpos < l_valid)
        keys[kt] = jnp.where(adm, key, jnp.int32(INT_MIN))
        return carry

    lax.fori_loop(0, nt, score_body, 0)

    def count_ge(cand):
        def body(kt, acc):
            ge = jnp.where(keys[kt] >= cand, 1.0, 0.0)
            part = ge[:, 0:LANE]
            for c in range(1, KT // LANE):
                part = part + ge[:, c * LANE:(c + 1) * LANE]
            return acc + part
        acc = lax.fori_loop(0, nt, body, jnp.zeros((TQ, LANE), F32))
        return jnp.sum(acc, axis=-1, keepdims=True)

    def bit_body(i, thr):
        bit = 31 - i
        cand = jnp.where(i == 0, jnp.zeros_like(thr), thr | jnp.left_shift(jnp.int32(1), bit))
        return jnp.where(count_ge(cand) >= float(n_top), cand, thr)

    thr = lax.fori_loop(0, 32, bit_body, jnp.full((TQ, 1), INT_MIN, jnp.int32))
    thr = jnp.maximum(thr, jnp.int32(INT_MIN + 1))

    q = q_ref[0]
    G = HEAD_GROUP
    qgs = [jnp.concatenate([q[:, (G * g + r) * HEAD_DIM:(G * g + r + 1) * HEAD_DIM] for r in range(G)],
                           axis=0) for g in range(KV_HEADS)]

    def scores(kt, g):
        start = pl.multiple_of(kt * KT, KT)
        kt_ = kb_ref[0, pl.ds(start, KT), g * HEAD_DIM:(g + 1) * HEAD_DIM]
        s = lax.dot_general(qgs[g], kt_, (((1,), (1,)), ((), ())), preferred_element_type=F32)
        return s.reshape(G, TQ, KT)

    def lane_fold(x, op):
        part = x[:, 0:LANE]
        for c in range(1, KT // LANE):
            part = op(part, x[:, c * LANE:(c + 1) * LANE])
        return part

    m_scr[...] = jnp.full((KV_HEADS, G * TQ, LANE), NEG_BIG, F32)

    def max_body(kt, carry):
        sel = (keys[kt] >= thr)[None]
        for g in range(KV_HEADS):
            s = jnp.where(sel, scores(kt, g), NEG_BIG).reshape(G * TQ, KT)
            m_scr[g] = jnp.maximum(m_scr[g], lane_fold(s, jnp.maximum))
        return carry

    lax.fori_loop(0, nt, max_body, 0)
    ms = [jnp.max(m_scr[g], axis=-1, keepdims=True).reshape(G, TQ, 1) for g in range(KV_HEADS)]
    l_scr[...] = jnp.zeros((KV_HEADS, G * TQ, LANE), F32)
    acc_scr[...] = jnp.zeros((KV_HEADS, G * TQ, HEAD_DIM), F32)

    def pv_body(kt, carry):
        start = pl.multiple_of(kt * KT, KT)
        sel = (keys[kt] >= thr)[None]
        for g in range(KV_HEADS):
            vt_ = vb_ref[0, pl.ds(start, KT), g * HEAD_DIM:(g + 1) * HEAD_DIM]
            p = jnp.where(sel, jnp.exp2(scores(kt, g) - ms[g]), 0.0).reshape(G * TQ, KT)
            l_scr[g] += lane_fold(p, jnp.add)
            acc_scr[g] += jnp.dot(p.astype(BF), vt_, preferred_element_type=F32)
        return carry

    lax.fori_loop(0, nt, pv_body, 0)
    for g in range(KV_HEADS):
        og = acc_scr[g] / jnp.sum(l_scr[g], axis=-1, keepdims=True)
        for r in range(G):
            o_ref[0, :, (G * g + r) * HEAD_DIM:(G * g + r + 1) * HEAD_DIM] = og[r * TQ:(r + 1) * TQ].astype(BF)


def _acore_call(q, qi, wi, kib, kb, vb, TQ, KT, pos0, l_valid):
    NB, TT, _ = q.shape
    LP = kb.shape[1]
    NT = LP // KT
    n_top = min(TOPK, l_valid // 4)
    res = lambda w: pl.BlockSpec((1, LP, w), lambda b, j: (b, 0, 0), pipeline_mode=pl.Buffered(1))
    blk = lambda w: pl.BlockSpec((1, TQ, w), lambda b, j: (b, j, 0))
    G = HEAD_GROUP
    return pl.pallas_call(
        functools.partial(_acore_kernel, TQ=TQ, KT=KT, NT=NT, pos0=pos0, l_valid=l_valid, n_top=n_top),
        grid=(NB, TT // TQ),
        in_specs=[blk(Q_W), blk(IDXQ_W), blk(IDX_HEADS), res(IDX_DIM), res(KV_W), res(KV_W)],
        out_specs=blk(Q_W),
        out_shape=jax.ShapeDtypeStruct((NB, TT, Q_W), BF),
        scratch_shapes=[pltpu.VMEM((NT, TQ, KT), jnp.int32), pltpu.VMEM((KV_HEADS, G * TQ, LANE), F32),
                        pltpu.VMEM((KV_HEADS, G * TQ, LANE), F32),
                        pltpu.VMEM((KV_HEADS, G * TQ, HEAD_DIM), F32)],
        compiler_params=_cparams(("arbitrary", "arbitrary"), 56),
        name="attn_core",
    )(q, qi, wi, kib, kb, vb)


def _trunk(x, mods, S, R, TF, TN, W, pos0, past):
    NB, TT, _ = x.shape
    fresh = past is None
    states = {}
    for i in range(DEPTH):
        kind = i % 4
        mod = mods[i]
        gn = W["g_norm"][i]
        x = _ffn_call(x, mod, gn[0:1], W["wg"], W["wu"], W["wd"], i, 0, 0, S, R, TF)
        if kind == 0:
            prefix = None if fresh else jnp.pad(past["pool"], ((0, 0), (1, 0), (0, 0)))
            x, st = _pool_call(x, mod, gn[1:2], W["w_pool"], W["s_pool"], R if fresh else TT, prefix)
            states["pool"] = st[:, 1:]
        elif kind == 1:
            prefix = None if fresh else jnp.pad(past["sconv"], ((0, 0), (SCONV_PAD - 2, 0), (0, 0)))
            z, st = _conv_call(x, mod, gn[1:2], W["w_sc_in"], "sconv", S, R, TN, W["w_sc_conv"],
                               prefix=prefix)
            states["sconv"] = st
            x = _linres_call(z, x, mod, 5, W["w_sc_out"], S, R, TN)
        elif kind == 2:
            reps = 1 if fresh else NB
            tabs = _rope_tables(pos0 + jnp.arange(TT), reps)
            q, k32, kb, v32, vb, qi, ki32, kib, wi = _aproj_call(
                x, mod, gn[1:2], W["w_attn_main"], W["w_attn_tail"], tabs, S, R)
            states["k"], states["v"], states["kidx"] = k32, v32, ki32
            if fresh:
                o = _acore_call(q, qi, wi, kib, kb, vb, 128, 512, pos0, TT)
            else:
                KT = 512
                L = past["k"].shape[1] + TT
                LP = -(-L // KT) * KT

                def cat(old, new):
                    return jnp.pad(jnp.concatenate([old.astype(BF), new], axis=1),
                                   ((0, 0), (0, LP - L), (0, 0)))

                o = _acore_call(q, qi, wi, cat(past["kidx"], kib), cat(past["k"], kb),
                                cat(past["v"], vb), TT, KT, pos0, L)
            x = _linres_call(o, x, mod, 5, W["w_attn_out"], S, R, TN)
        else:
            prefix = None if fresh else jnp.pad(past["cconv"], ((0, 0), (CCONV_PAD - 30, 0), (0, 0)))
            cv, st = _conv_call(x, mod, gn[1:2], W["w_cm_pw1"], "cconv", S, R, TN, W["w_cm_dw"],
                                pbias=W["b_cm_pw1"], b_dw=W["b_cm_dw"], prefix=prefix)
            states["cconv"] = st
            x = _linres_call(cv, x, mod, 5, W["w_cm_pw2"], S, R, TN,
                             ln=(W["g_cm_ln"], W["b_cm_ln"]), bias=W["b_cm_pw2"])
        gfin = W["g_final"] if i == DEPTH - 1 else None
        x = _ffn_call(x, mod, gn[2:3], W["wg"], W["wu"], W["wd"], i, 1, 2, S, R, TF, gfin)
    return x, states


def kernel(x_prompt, x_sample, state_pool, state_sconv, cache_k, cache_v, cache_kidx, state_cconv, c_prompt, c_sample, w_mod, b_mod, g_norm, w_ffn_gate, w_ffn_up, w_ffn_down, w_pool, s_pool, w_sc_in, w_sc_conv, w_sc_out, w_attn_in, w_attn_out, w_cm_pw1, b_cm_pw1, w_cm_dw, b_cm_dw, g_cm_ln, b_cm_ln, w_cm_pw2, b_cm_pw2, g_final):
    B, T, _ = x_prompt.shape
    NS, TS, _ = x_sample.shape
    assert D_FF % 512 == 0 and T % 512 == 0 and B + NS <= MOD_ROWS
    assert w_pool.shape[0] == w_sc_in.shape[0] == w_attn_in.shape[0] == w_cm_pw1.shape[0] == 1

    c_all = jnp.concatenate([c_prompt, c_sample, jnp.zeros((MOD_ROWS - B - NS, D), F32)], axis=0)
    mod_all = _mod_call(c_all, w_mod, b_mod)
    mods_p = [mod_all[i, 0:B].reshape(B, 1, 9 * D) for i in range(DEPTH)]
    mods_s = [mod_all[i, B:B + NS].reshape(NS, 1, 9 * D) for i in range(DEPTH)]

    wa = w_attn_in[0].astype(BF)
    W = dict(
        g_norm=g_norm, g_final=g_final.reshape(1, D),
        wg=w_ffn_gate.astype(BF), wu=w_ffn_up.astype(BF), wd=w_ffn_down.astype(BF),
        w_pool=w_pool[0].astype(BF), s_pool=s_pool,
        w_sc_in=w_sc_in[0].astype(BF), w_sc_conv=w_sc_conv[0], w_sc_out=w_sc_out[0].astype(BF),
        w_attn_main=wa, w_attn_tail=jnp.pad(wa[:, ATTN_MAIN:], ((0, 0), (0, LANE - ATTN_TAIL))),
        w_attn_out=w_attn_out[0].astype(BF),
        w_cm_pw1=w_cm_pw1[0].astype(BF), b_cm_pw1=b_cm_pw1, w_cm_dw=w_cm_dw[0], b_cm_dw=b_cm_dw,
        g_cm_ln=g_cm_ln, b_cm_ln=b_cm_ln, w_cm_pw2=w_cm_pw2[0].astype(BF), b_cm_pw2=b_cm_pw2,
    )

    y_p, st_p = _trunk(x_prompt, mods_p, 1, 512, 512, 512, W, 0, None)
    past = dict(pool=state_pool[0], sconv=state_sconv[0], cconv=state_cconv[0],
                k=cache_k[0].reshape(NS, -1, KV_W), v=cache_v[0].reshape(NS, -1, KV_W),
                kidx=cache_kidx[0])
    y_s, st_s = _trunk(x_sample, mods_s, NS, TS, 512, 512, W, cache_k.shape[2], past)

    def kv(a):
        return a.reshape(1, a.shape[0], a.shape[1], KV_HEADS, HEAD_DIM)

    return (y_p, y_s, st_p["pool"][None], st_s["pool"][None], st_p["sconv"][None], st_s["sconv"][None],
            kv(st_p["k"]), kv(st_s["k"]), kv(st_p["v"]), kv(st_s["v"]),
            st_p["kidx"][None], st_s["kidx"][None], st_p["cconv"][None], st_s["cconv"][None])
```

```python
import functools
import math

import jax
import jax.numpy as jnp
import numpy as np
from jax import lax
from jax.experimental import pallas as pl
from jax.experimental.pallas import tpu as pltpu

D = 2048
D_FF = 5632
DEPTH = 4
NORM_EPS = 1e-6
CHUNK = 64
POOL_WINDOWS = (2, 4, 8, 16)
POOL_GROUP = D // 4
POOL_PAD = 16
SCONV_WIDTH = 3
SCONV_PAD = 8
CCONV_WIDTH = 31
CCONV_PAD = 32
N_HEADS = 16
HEAD_DIM = 128
KV_HEADS = 4
HEAD_GROUP = N_HEADS // KV_HEADS
ROPE_DIM = HEAD_DIM // 4
ROPE_THETA = 500000.0
IDX_HEADS = 16
IDX_DIM = 64
IDX_ROPE_DIM = IDX_DIM // 4
IDX_SCALE = IDX_HEADS ** -0.5 * IDX_DIM ** -0.5
TOPK = 256
Q_W = N_HEADS * HEAD_DIM
KV_W = KV_HEADS * HEAD_DIM
IDXQ_W = IDX_HEADS * IDX_DIM
ATTN_MAIN = Q_W + 2 * KV_W + IDXQ_W
ATTN_TAIL = IDX_DIM + IDX_HEADS
LANE = 128
MOD_ROWS = 40
Q_SCALE = HEAD_DIM ** -0.5 * math.log2(math.e)
INT_MIN = -2 ** 31
NEG_BIG = -1e30

BF = jnp.bfloat16
F32 = jnp.float32


BF16_ROWS = 16
SUBLANES = 8
CONV_CHUNK_ROWS = 64
CONV_CHUNK_LANES = 256


def _round_up(n, m):
    return -(-n // m) * m


def _cparams(sem, vmem_mib):
    return pltpu.CompilerParams(dimension_semantics=sem, vmem_limit_bytes=vmem_mib << 20)


def _rms(x, g):
    return x * lax.rsqrt(jnp.mean(x * x, axis=-1, keepdims=True) + NORM_EPS) * g


def _rms_mod(x, g, shift, scale):
    return _rms(x, g) * (1.0 + scale) + shift


def _silu(x):
    return x * jax.nn.sigmoid(x)


def _row_spec(S, R, W):
    return pl.BlockSpec((S, R, W), lambda b, t, n: (b, t, 0))


def _mod_spec(S, col):
    return pl.BlockSpec((S, 1, D), lambda b, t, n: (b, 0, col))


def _const_spec(shape):
    nd = len(shape)
    return pl.BlockSpec(shape, lambda b, t, n: (0,) * nd)


def _mod_kernel(c_ref, w_ref, b_ref, o_ref):
    a = _silu(c_ref[...]).astype(BF)
    o_ref[0] = jnp.dot(a, w_ref[0].astype(BF), preferred_element_type=F32) + b_ref[0]


def _mod_call(c_all, w_mod, b_mod):
    TN = 1024
    N = 9 * D
    return pl.pallas_call(
        _mod_kernel,
        grid=(DEPTH, N // TN),
        in_specs=[pl.BlockSpec((MOD_ROWS, D), lambda i, n: (0, 0)),
                  pl.BlockSpec((1, D, TN), lambda i, n: (i, 0, n)),
                  pl.BlockSpec((1, 1, TN), lambda i, n: (i, 0, n))],
        out_specs=pl.BlockSpec((1, MOD_ROWS, TN), lambda i, n: (i, 0, n)),
        out_shape=jax.ShapeDtypeStruct((DEPTH, MOD_ROWS, N), F32),
        compiler_params=_cparams(("arbitrary", "arbitrary"), 40),
        name="mod",
    )(c_all, w_mod, b_mod.reshape(DEPTH, 1, N))


def _ffn_kernel(*refs, S, R, final):
    if final:
        x_ref, sh_ref, sc_ref, gt_ref, g_ref, wg_ref, wu_ref, wd_ref, gf_ref, o_ref, h_scr = refs
    else:
        x_ref, sh_ref, sc_ref, gt_ref, g_ref, wg_ref, wu_ref, wd_ref, o_ref, h_scr = refs
    f = pl.program_id(2)
    M = S * R

    @pl.when(f == 0)
    def _():
        h = _rms_mod(x_ref[...], g_ref[...], sh_ref[...], sc_ref[...])
        h_scr[...] = h.reshape(M, D).astype(BF)
        o_ref[...] = jnp.zeros_like(o_ref)

    h = h_scr[...]
    g = jnp.dot(h, wg_ref[...], preferred_element_type=F32)
    u = jnp.dot(h, wu_ref[...], preferred_element_type=F32)
    a = (_silu(g) * u).astype(BF)
    o_ref[...] += jnp.dot(a, wd_ref[...], preferred_element_type=F32).reshape(S, R, D)

    @pl.when(f == pl.num_programs(2) - 1)
    def _():
        y = x_ref[...] + 0.5 * gt_ref[...] * o_ref[...]
        if final:
            y = _rms(y, gf_ref[...])
        o_ref[...] = y


def _ffn_call(x, mod, g, wg, wu, wd, layer, which, sub, S, R, TF, gfin=None):
    NB, TT, _ = x.shape
    final = gfin is not None
    in_specs = [_row_spec(S, R, D), _mod_spec(S, 3 * sub), _mod_spec(S, 3 * sub + 1),
                _mod_spec(S, 3 * sub + 2), _const_spec((1, D)),
                pl.BlockSpec((None, None, D, TF), lambda b, t, f: (layer, which, 0, f)),
                pl.BlockSpec((None, None, D, TF), lambda b, t, f: (layer, which, 0, f)),
                pl.BlockSpec((None, None, TF, D), lambda b, t, f: (layer, which, f, 0))]
    args = [x, mod, mod, mod, g, wg, wu, wd]
    if final:
        in_specs.append(_const_spec((1, D)))
        args.append(gfin)
    return pl.pallas_call(
        functools.partial(_ffn_kernel, S=S, R=R, final=final),
        grid=(NB // S, TT // R, D_FF // TF),
        in_specs=in_specs,
        out_specs=_row_spec(S, R, D),
        out_shape=jax.ShapeDtypeStruct(x.shape, F32),
        scratch_shapes=[pltpu.VMEM((S * R, D), BF)],
        compiler_params=_cparams(("arbitrary", "arbitrary", "arbitrary"), 48),
        name="ffn",
    )(*args)


def _linres_kernel(*refs, S, R, ln, bias):
    refs = list(refs)
    a_ref, x_ref, gt_ref, w_ref = refs[:4]
    rest = refs[4:]
    if ln:
        gl_ref, bl_ref = rest[:2]
        rest = rest[2:]
    if bias:
        b_ref = rest[0]
        rest = rest[1:]
    (o_ref,) = rest
    a = a_ref[...]
    if ln:
        a = a.astype(F32)
        mu = jnp.mean(a, axis=-1, keepdims=True)
        ac = a - mu
        a = ac * lax.rsqrt(jnp.mean(ac * ac, axis=-1, keepdims=True) + NORM_EPS)
        a = _silu(a * gl_ref[...] + bl_ref[...])
    a = a.reshape(S * R, a.shape[-1]).astype(BF)
    y = jnp.dot(a, w_ref[...], preferred_element_type=F32)
    if bias:
        y = y + b_ref[...]
    o_ref[...] = x_ref[...] + gt_ref[...] * y.reshape(o_ref.shape)


def _linres_call(a, x, mod, gate_col, w, S, R, ln=None, bias=None):
    NB, TT, K = a.shape
    in_specs = [_row_spec(S, R, K), _row_spec(S, R, D), _mod_spec(S, gate_col),
                pl.BlockSpec((K, D), lambda b, t, n: (0, 0), pipeline_mode=pl.Buffered(1))]
    args = [a, x, mod, w]
    if ln is not None:
        in_specs += [_const_spec((1, K)), _const_spec((1, K))]
        args += list(ln)
    if bias is not None:
        in_specs.append(_const_spec((1, D)))
        args.append(bias)
    return pl.pallas_call(
        functools.partial(_linres_kernel, S=S, R=R, ln=ln is not None, bias=bias is not None),
        grid=(NB // S, TT // R, 1),
        in_specs=in_specs,
        out_specs=_row_spec(S, R, D),
        out_shape=jax.ShapeDtypeStruct(x.shape, F32),
        compiler_params=_cparams(("arbitrary", "arbitrary", "arbitrary"), 56),
        name="linres",
    )(*args)


def _pool_kernel(*refs, R, has_prefix, n_valid_prefix):
    if has_prefix:
        (x_ref, sh_ref, sc_ref, gt_ref, g_ref, wp_ref, sp_ref, pre_ref,
         o_ref, st_ref, hext) = refs
    else:
        (x_ref, sh_ref, sc_ref, gt_ref, g_ref, wp_ref, sp_ref,
         o_ref, st_ref, hext, carry) = refs
    t = pl.program_id(1)
    x = x_ref[0]
    h = _rms_mod(x, g_ref[...], sh_ref[0], sc_ref[0])
    if has_prefix:
        hext[0:POOL_PAD, :] = pre_ref[0]
    else:
        @pl.when(t == 0)
        def _():
            hext[0:POOL_PAD, :] = jnp.zeros((POOL_PAD, D), F32)

        @pl.when(t > 0)
        def _():
            hext[0:POOL_PAD, :] = carry[...]

        carry[...] = h[R - POOL_PAD:R, :]
    hext[POOL_PAD:POOL_PAD + R, :] = h
    st_ref[0] = h[R - POOL_PAD:R, :]
    tpos = t * R + lax.broadcasted_iota(jnp.int32, (R, 1), 0)
    gate = gt_ref[0]
    for gi, w in enumerate(POOL_WINDOWS):
        lo = gi * POOL_GROUP
        hi = lo + POOL_GROUP
        rc = min(R, CONV_CHUNK_ROWS)
        wins = []
        for r0 in range(0, R, rc):
            acc = hext[POOL_PAD + r0:POOL_PAD + r0 + rc, lo:hi]
            for j in range(1, w):
                acc = acc + hext[POOL_PAD + r0 - j:POOL_PAD + r0 - j + rc, lo:hi]
            wins.append(acc)
        win = jnp.concatenate(wins, axis=0) if len(wins) > 1 else wins[0]
        cnt = jnp.minimum(tpos + 1 + n_valid_prefix, w).astype(F32)
        pooled = (win / cnt - h[:, lo:hi]).astype(BF)
        y = jnp.dot(pooled, wp_ref[gi], preferred_element_type=F32) * sp_ref[:, lo:hi]
        o_ref[0, :, lo:hi] = x[:, lo:hi] + gate[:, lo:hi] * y


def _pool_call(x, mod, g, w_pool, s_pool, R, prefix=None):
    NB, TT, _ = x.shape
    has_prefix = prefix is not None
    in_specs = [_row_spec(1, R, D), _mod_spec(1, 3), _mod_spec(1, 4), _mod_spec(1, 5),
                _const_spec((1, D)), _const_spec((4, POOL_GROUP, POOL_GROUP)), _const_spec((1, D))]
    args = [x, mod, mod, mod, g, w_pool, s_pool]
    scratch = [pltpu.VMEM((R + POOL_PAD, D), F32)]
    if has_prefix:
        in_specs.append(pl.BlockSpec((1, POOL_PAD, D), lambda b, t, n: (b, 0, 0)))
        args.append(prefix)
    else:
        scratch.append(pltpu.VMEM((POOL_PAD, D), F32))
    return pl.pallas_call(
        functools.partial(_pool_kernel, R=R, has_prefix=has_prefix,
                          n_valid_prefix=POOL_PAD - 1 if has_prefix else 0),
        grid=(NB, TT // R, 1),
        in_specs=in_specs,
        out_specs=[_row_spec(1, R, D), pl.BlockSpec((1, POOL_PAD, D), lambda b, t, n: (b, 0, 0))],
        out_shape=[jax.ShapeDtypeStruct(x.shape, F32), jax.ShapeDtypeStruct((NB, POOL_PAD, D), F32)],
        scratch_shapes=scratch,
        compiler_params=_cparams(("arbitrary", "arbitrary", "arbitrary"), 40),
        name="pool",
    )(*args)


def _conv_kernel(*refs, S, R, TN, mode, has_prefix):
    refs = list(refs)
    x_ref, sh_ref, sc_ref, g_ref = refs[:4]
    rest = refs[4:]
    nproj = 3 if mode == "sconv" else 2
    w_refs = rest[:nproj]
    rest = rest[nproj:]
    if mode == "cconv":
        pb_refs = rest[:2]
        wc_ref, bdw_ref = rest[2:4]
        rest = rest[4:]
        width, pad = CCONV_WIDTH, CCONV_PAD
    else:
        wc_ref = rest[0]
        rest = rest[1:]
        width, pad = SCONV_WIDTH, SCONV_PAD
    if has_prefix:
        pre_ref = rest[0]
        z_ref, st_ref, h_scr, uext, vbuf = rest[1:]
    else:
        z_ref, st_ref, h_scr, uext, vbuf, carry = rest
    t = pl.program_id(1)
    n = pl.program_id(2)
    M = S * R

    @pl.when(n == 0)
    def _():
        h = _rms_mod(x_ref[...], g_ref[...], sh_ref[...], sc_ref[...])
        h_scr[...] = h.reshape(M, D).astype(BF)

    h = h_scr[...]
    proj = [jnp.dot(h, w_ref[...], preferred_element_type=F32) for w_ref in w_refs]
    if mode == "sconv":
        bq, cq, vq = proj
        u = cq * vq
    else:
        a = proj[0] + pb_refs[0][...]
        gq = proj[1] + pb_refs[1][...]
        u = a * jax.nn.sigmoid(gq)
    u3 = u.reshape(S, R, TN)
    if has_prefix:
        uext[:, 0:pad, :] = pre_ref[...]
    else:
        @pl.when(t == 0)
        def _():
            uext[:, 0:pad, :] = jnp.zeros((S, pad, TN), F32)

        @pl.when(t > 0)
        def _():
            uext[:, 0:pad, :] = carry[n]
    uext[:, pad:pad + R, :] = u3
    tail = uext[:, R:R + pad, :]
    if not has_prefix:
        carry[n] = tail
    st_ref[...] = tail
    wc = wc_ref[...]
    if mode == "sconv":
        bq3 = bq.reshape(S, R, TN)
    base = pad - (width - 1)
    rc = min(R, CONV_CHUNK_ROWS)
    sc_ = max(1, CONV_CHUNK_ROWS // rc)
    for s0 in range(0, S, sc_):
        for r0 in range(0, R, rc):
            for l0 in range(0, TN, CONV_CHUNK_LANES):
                ls = slice(l0, l0 + CONV_CHUNK_LANES)
                acc = None
                for sub in range(min(SUBLANES, width)):
                    taps = range(sub, width, SUBLANES)
                    start = r0 + base + sub
                    span = rc + taps[-1] - sub
                    vbuf[:, 0:span, :] = uext[s0:s0 + sc_, start:start + span, ls]
                    for j in taps:
                        term = wc[j:j + 1, ls] * vbuf[:, j - sub:j - sub + rc, :]
                        acc = term if acc is None else acc + term
                if mode == "sconv":
                    z_ref[s0:s0 + sc_, r0:r0 + rc, ls] = (bq3[s0:s0 + sc_, r0:r0 + rc, ls] * acc).astype(z_ref.dtype)
                else:
                    z_ref[s0:s0 + sc_, r0:r0 + rc, ls] = acc + bdw_ref[:, ls]


def _conv_call(x, mod, g, w, mode, S, R, TN, w_conv, pbias=None, b_dw=None, prefix=None):
    NB, TT, _ = x.shape
    has_prefix = prefix is not None
    nproj = 3 if mode == "sconv" else 2
    width, pad = (SCONV_WIDTH, SCONV_PAD) if mode == "sconv" else (CCONV_WIDTH, CCONV_PAD)
    nN = D // TN
    in_specs = [_row_spec(S, R, D), _mod_spec(S, 3), _mod_spec(S, 4), _const_spec((1, D))]
    args = [x, mod, mod, g]
    for k in range(nproj):
        in_specs.append(pl.BlockSpec((D, TN), lambda b, t, n, k=k: (0, k * nN + n)))
        args.append(w)
    if mode == "cconv":
        for k in range(2):
            in_specs.append(pl.BlockSpec((1, TN), lambda b, t, n, k=k: (0, k * nN + n)))
            args.append(pbias)
    in_specs.append(pl.BlockSpec((width, TN), lambda b, t, n: (0, n)))
    args.append(w_conv)
    if mode == "cconv":
        in_specs.append(pl.BlockSpec((1, TN), lambda b, t, n: (0, n)))
        args.append(b_dw)
    rc = min(R, CONV_CHUNK_ROWS)
    scratch = [pltpu.VMEM((S * R, D), BF), pltpu.VMEM((S, R + pad, TN), F32),
               pltpu.VMEM((max(1, CONV_CHUNK_ROWS // rc), rc + _round_up(width, SUBLANES), CONV_CHUNK_LANES), F32)]
    if has_prefix:
        in_specs.append(pl.BlockSpec((S, pad, TN), lambda b, t, n: (b, 0, n)))
        args.append(prefix)
    else:
        scratch.append(pltpu.VMEM((nN, S, pad, TN), F32))
    zdt = BF if mode == "sconv" else F32
    z, tails = pl.pallas_call(
        functools.partial(_conv_kernel, S=S, R=R, TN=TN, mode=mode, has_prefix=has_prefix),
        grid=(NB // S, TT // R, nN),
        in_specs=in_specs,
        out_specs=[pl.BlockSpec((S, R, TN), lambda b, t, n: (b, t, n)),
                   pl.BlockSpec((S, pad, TN), lambda b, t, n: (b, t, n))],
        out_shape=[jax.ShapeDtypeStruct(x.shape, zdt),
                   jax.ShapeDtypeStruct((NB, (TT // R) * pad, D), F32)],
        scratch_shapes=scratch,
        compiler_params=_cparams(("arbitrary", "arbitrary", "arbitrary"), 40),
        name=mode,
    )(*args)
    return z, tails[:, -(width - 1):]


def _rope_lanes(x, cos, sin, period, half):
    lane = lax.broadcasted_iota(jnp.int32, x.shape, x.ndim - 1) % period
    width = x.shape[-1]
    partner = jnp.where(lane < half, pltpu.roll(x, width - half, x.ndim - 1),
                        pltpu.roll(x, half, x.ndim - 1))
    return x * cos + partner * sin


def _aproj_kernel(x_ref, sh_ref, sc_ref, g_ref, w_ref, wt_ref,
                  ch_ref, sh_h_ref, ci_ref, si_ref, ct_ref, st_ref,
                  q_ref, k32_ref, kb_ref, v32_ref, vb_ref, qi_ref, ki32_ref, kib_ref, wi_ref,
                  h_scr, *, S, R):
    n = pl.program_id(2)
    M = S * R
    TN = 4 * HEAD_DIM

    @pl.when(n == 0)
    def _():
        h = _rms_mod(x_ref[...], g_ref[...], sh_ref[...], sc_ref[...])
        h_scr[...] = h.reshape(M, D).astype(BF)

    def heads_rope(p):
        cos, sin = ch_ref[...], sh_h_ref[...]
        return jnp.concatenate(
            [_rope_lanes(p[:, i * HEAD_DIM:(i + 1) * HEAD_DIM], cos, sin, HEAD_DIM, ROPE_DIM // 2)
             for i in range(TN // HEAD_DIM)], axis=-1)

    @pl.when(n < 4)
    def _():
        p = jnp.dot(h_scr[...], w_ref[...], preferred_element_type=F32)
        q = heads_rope(p) * Q_SCALE
        q_ref[...] = q.reshape(S, R, TN).astype(BF)

    @pl.when(n == 4)
    def _():
        p = jnp.dot(h_scr[...], w_ref[...], preferred_element_type=F32)
        k = heads_rope(p).reshape(S, R, TN)
        k32_ref[...] = k
        kb_ref[...] = k.astype(BF)

    @pl.when(n == 5)
    def _():
        v = jnp.dot(h_scr[...], w_ref[...], preferred_element_type=F32).reshape(S, R, TN)
        v32_ref[...] = v
        vb_ref[...] = v.astype(BF)

    @pl.when((n == 6) | (n == 7))
    def _():
        p = jnp.dot(h_scr[...], w_ref[...], preferred_element_type=F32)
        cos, sin = ci_ref[...], si_ref[...]
        qi = jnp.concatenate(
            [_rope_lanes(p[:, i * LANE:(i + 1) * LANE], cos, sin, IDX_DIM, IDX_ROPE_DIM // 2)
             for i in range(TN // LANE)], axis=-1)
        qi_ref[...] = qi.reshape(S, R, TN).astype(BF)

    @pl.when(n == 8)
    def _():
        p = jnp.dot(h_scr[...], wt_ref[...], preferred_element_type=F32)
        tl = _rope_lanes(p, ct_ref[...], st_ref[...], LANE, IDX_ROPE_DIM // 2)
        ki = tl[:, 0:IDX_DIM].reshape(S, R, IDX_DIM)
        ki32_ref[...] = ki
        kib_ref[...] = ki.astype(BF)
        wi_ref[...] = (tl[:, IDX_DIM:IDX_DIM + IDX_HEADS] * IDX_SCALE).reshape(S, R, IDX_HEADS)


def _aproj_call(x, mod, g, w_main, w_tail, tabs, S, R):
    NB, TT, _ = x.shape
    M = S * R
    TN = 4 * HEAD_DIM
    tab_spec = pl.BlockSpec((M, LANE), lambda b, t, n: (t, 0))

    def out(width, dt, idx=lambda b, t, n: (b, t, 0)):
        return (pl.BlockSpec((S, R, min(width, TN)), idx), jax.ShapeDtypeStruct((NB, TT, width), dt))

    outs = [out(Q_W, BF, lambda b, t, n: (b, t, jnp.minimum(n, 3))),
            out(KV_W, F32), out(KV_W, BF), out(KV_W, F32), out(KV_W, BF),
            out(IDXQ_W, BF, lambda b, t, n: (b, t, jnp.clip(n - 6, 0, 1))),
            out(IDX_DIM, F32), out(IDX_DIM, BF), out(IDX_HEADS, F32)]
    return pl.pallas_call(
        functools.partial(_aproj_kernel, S=S, R=R),
        grid=(NB // S, TT // R, 9),
        in_specs=[_row_spec(S, R, D), _mod_spec(S, 3), _mod_spec(S, 4), _const_spec((1, D)),
                  pl.BlockSpec((D, TN), lambda b, t, n: (0, jnp.minimum(n, ATTN_MAIN // TN - 1))),
                  _const_spec((D, LANE))] + [tab_spec] * 6,
        out_specs=[o[0] for o in outs],
        out_shape=[o[1] for o in outs],
        scratch_shapes=[pltpu.VMEM((M, D), BF)],
        compiler_params=_cparams(("arbitrary", "arbitrary", "arbitrary"), 40),
        name="attn_proj",
    )(x, mod, mod, g, w_main, w_tail, *tabs)


def _rope_tables(pos, reps):
    def tab(rot_dim, period, lanes_used):
        half = rot_dim // 2
        inv = jnp.exp(-math.log(ROPE_THETA) * jnp.arange(half, dtype=F32) * (2.0 / rot_dim))
        ang = pos.astype(F32)[:, None] * inv[None, :]
        cos, sin = jnp.cos(ang), jnp.sin(ang)
        n = pos.shape[0]
        one = jnp.ones((n, period - rot_dim), F32)
        zero = jnp.zeros((n, period - rot_dim), F32)
        c = jnp.concatenate([cos, cos, one], axis=1)
        s = jnp.concatenate([-sin, sin, zero], axis=1)
        c = jnp.tile(c, (1, lanes_used // period))
        s = jnp.tile(s, (1, lanes_used // period))
        if lanes_used < LANE:
            c = jnp.concatenate([c, jnp.ones((n, LANE - lanes_used), F32)], axis=1)
            s = jnp.concatenate([s, jnp.zeros((n, LANE - lanes_used), F32)], axis=1)
        return [jnp.tile(c, (reps, 1)), jnp.tile(s, (reps, 1))]

    return tab(ROPE_DIM, HEAD_DIM, LANE) + tab(IDX_ROPE_DIM, IDX_DIM, LANE) + tab(IDX_ROPE_DIM, IDX_DIM, IDX_DIM)


def _topk_threshold(count, rewrite, rows, n_top, pos_bits):
    n_top = float(n_top)

    def bit_body(i, thr):
        cand = jnp.where(i == 0, jnp.zeros_like(thr), thr | jnp.left_shift(jnp.int32(1), 31 - i))
        return jnp.where(count(lambda k, p: k >= cand) >= n_top, cand, thr)

    thr = lax.fori_loop(0, 32, bit_body, jnp.full((rows, 1), INT_MIN, jnp.int32))
    excess = (count(lambda k, p: k >= thr) > n_top) & (thr > INT_MIN)

    @pl.when(jnp.max(jnp.where(excess, 1, 0)) > 0)
    def _():
        quota = n_top - count(lambda k, p: k > thr)

        def pos_body(i, cut):
            cand = cut | jnp.left_shift(jnp.int32(1), pos_bits - 1 - i)
            return jnp.where(count(lambda k, p: (k == thr) & (p < cand)) < quota, cand, cut)

        cut = lax.fori_loop(0, pos_bits, pos_body, jnp.zeros((rows, 1), jnp.int32))
        rewrite(lambda k, p: jnp.where((k == thr) & (p > cut), jnp.int32(INT_MIN), k))

    return jnp.maximum(thr, jnp.int32(INT_MIN + 1))


def _acore_kernel(q_ref, qi_ref, wi_ref, ki_ref, kb_ref, vb_ref, o_ref,
                  keys, m_scr, l_scr, acc_scr, *, TQ, KT, NT, pos0, l_valid, n_top):
    j = pl.program_id(1)
    q0 = pos0 + j * TQ
    last_key = ((q0 + TQ - 1) // CHUNK) * CHUNK + CHUNK - 1
    nt = jnp.minimum(last_key // KT + 1, NT)
    qchunk = (q0 + lax.broadcasted_iota(jnp.int32, (TQ, 1), 0)) // CHUNK

    qi = qi_ref[0]
    qst = jnp.concatenate([qi[:, h * IDX_DIM:(h + 1) * IDX_DIM] for h in range(IDX_HEADS)], axis=0)
    wi = wi_ref[0]
    wst = jnp.concatenate([wi[:, h:h + 1] for h in range(IDX_HEADS)], axis=0)

    def score_body(kt, carry):
        start = pl.multiple_of(kt * KT, KT)
        key = _score_keys(qst, wst, ki_ref[0, pl.ds(start, KT), :], TQ)
        kpos = start + lax.broadcasted_iota(jnp.int32, (1, KT), 1)
        adm = (kpos // CHUNK <= qchunk) & (kpos < l_valid)
        keys[kt] = jnp.where(adm, key, jnp.int32(INT_MIN))
        return carry

    lax.fori_loop(0, nt, score_body, 0)

    def tile_pos(kt):
        return kt * KT + lax.broadcasted_iota(jnp.int32, (1, KT), 1)

    def count(pred):
        def body(kt, acc):
            hit = jnp.where(pred(keys[kt], tile_pos(kt)), 1.0, 0.0)
            for c in range(KT // LANE):
                acc = acc + hit[:, c * LANE:(c + 1) * LANE]
            return acc
        acc = lax.fori_loop(0, nt, body, jnp.zeros((TQ, LANE), F32))
        return jnp.sum(acc, axis=-1, keepdims=True)

    def rewrite(fn):
        def body(kt, carry):
            keys[kt] = fn(keys[kt], tile_pos(kt))
            return carry
        lax.fori_loop(0, nt, body, 0)

    thr = _topk_threshold(count, rewrite, TQ, n_top, (NT * KT).bit_length())

    q = q_ref[0]
    G = HEAD_GROUP
    qgs = [jnp.concatenate([q[:, (G * g + r) * HEAD_DIM:(G * g + r + 1) * HEAD_DIM] for r in range(G)],
                           axis=0) for g in range(KV_HEADS)]

    def scores(kt, g):
        start = pl.multiple_of(kt * KT, KT)
        kt_ = kb_ref[0, pl.ds(start, KT), g * HEAD_DIM:(g + 1) * HEAD_DIM]
        s = lax.dot_general(qgs[g], kt_, (((1,), (1,)), ((), ())), preferred_element_type=F32)
        return s.reshape(G, TQ, KT)

    def lane_fold(x, op):
        part = x[:, 0:LANE]
        for c in range(1, KT // LANE):
            part = op(part, x[:, c * LANE:(c + 1) * LANE])
        return part

    m_scr[...] = jnp.full((KV_HEADS, G * TQ, LANE), NEG_BIG, F32)

    def max_body(kt, carry):
        sel = (keys[kt] >= thr)[None]
        for g in range(KV_HEADS):
            s = jnp.where(sel, scores(kt, g), NEG_BIG).reshape(G * TQ, KT)
            m_scr[g] = jnp.maximum(m_scr[g], lane_fold(s, jnp.maximum))
        return carry

    lax.fori_loop(0, nt, max_body, 0)
    ms = [jnp.max(m_scr[g], axis=-1, keepdims=True).reshape(G, TQ, 1) for g in range(KV_HEADS)]
    l_scr[...] = jnp.zeros((KV_HEADS, G * TQ, LANE), F32)
    acc_scr[...] = jnp.zeros((KV_HEADS, G * TQ, HEAD_DIM), F32)

    def pv_body(kt, carry):
        start = pl.multiple_of(kt * KT, KT)
        sel = (keys[kt] >= thr)[None]
        for g in range(KV_HEADS):
            vt_ = vb_ref[0, pl.ds(start, KT), g * HEAD_DIM:(g + 1) * HEAD_DIM]
            p = jnp.where(sel, jnp.exp2(scores(kt, g) - ms[g]), 0.0).reshape(G * TQ, KT)
            l_scr[g] += lane_fold(p, jnp.add)
            acc_scr[g] += jnp.dot(p.astype(BF), vt_, preferred_element_type=F32)
        return carry

    lax.fori_loop(0, nt, pv_body, 0)
    for g in range(KV_HEADS):
        og = acc_scr[g] / jnp.sum(l_scr[g], axis=-1, keepdims=True)
        for r in range(G):
            o_ref[0, :, (G * g + r) * HEAD_DIM:(G * g + r + 1) * HEAD_DIM] = og[r * TQ:(r + 1) * TQ].astype(BF)


def _acore_call(q, qi, wi, kib, kb, vb, TQ, KT, pos0, l_valid):
    NB, TT, _ = q.shape
    LP = kb.shape[1]
    NT = LP // KT
    n_top = min(TOPK, l_valid // 4)
    res = lambda w: pl.BlockSpec((1, LP, w), lambda b, j: (b, 0, 0), pipeline_mode=pl.Buffered(1))
    blk = lambda w: pl.BlockSpec((1, TQ, w), lambda b, j: (b, j, 0))
    G = HEAD_GROUP
    return pl.pallas_call(
        functools.partial(_acore_kernel, TQ=TQ, KT=KT, NT=NT, pos0=pos0, l_valid=l_valid, n_top=n_top),
        grid=(NB, TT // TQ),
        in_specs=[blk(Q_W), blk(IDXQ_W), blk(IDX_HEADS), res(IDX_DIM), res(KV_W), res(KV_W)],
        out_specs=blk(Q_W),
        out_shape=jax.ShapeDtypeStruct((NB, TT, Q_W), BF),
        scratch_shapes=[pltpu.VMEM((NT, TQ, KT), jnp.int32), pltpu.VMEM((KV_HEADS, G * TQ, LANE), F32),
                        pltpu.VMEM((KV_HEADS, G * TQ, LANE), F32),
                        pltpu.VMEM((KV_HEADS, G * TQ, HEAD_DIM), F32)],
        compiler_params=_cparams(("arbitrary", "arbitrary"), 56),
        name="attn_core",
    )(q, qi, wi, kib, kb, vb)


def _score_keys(qst, wst, kmat, TQ):
    s = lax.dot_general(qst, kmat, (((1,), (1,)), ((), ())), preferred_element_type=F32)
    s = jnp.maximum(s, 0.0) * wst
    sc = jnp.sum(s.reshape(IDX_HEADS, TQ, kmat.shape[0]), axis=0) + 0.0
    bits = lax.bitcast_convert_type(sc, jnp.int32)
    return jnp.where(bits < 0, bits ^ jnp.int32(0x7FFFFFFF), bits)


def _dec_index_kernel(qi_ref, wi_ref, cki_ref, kin_ref, keys_ref, thr_ref, kscr, *, SB, TQ, LC, KT, pos0, n_top):
    rows = SB * TQ
    width = LC + LANE
    qchunk = (pos0 + lax.broadcasted_iota(jnp.int32, (TQ, 1), 0)) // CHUNK
    for s in range(SB):
        qi = qi_ref[s]
        qst = jnp.concatenate([qi[:, h * IDX_DIM:(h + 1) * IDX_DIM] for h in range(IDX_HEADS)], axis=0)
        wi = wi_ref[s]
        wst = jnp.concatenate([wi[:, h:h + 1] for h in range(IDX_HEADS)], axis=0)
        for c in range(LC // KT):
            kit = cki_ref[s, c * KT:(c + 1) * KT, :].astype(BF)
            kpos = c * KT + lax.broadcasted_iota(jnp.int32, (1, KT), 1)
            key = _score_keys(qst, wst, kit, TQ)
            kscr[s * TQ:(s + 1) * TQ, c * KT:(c + 1) * KT] = jnp.where(kpos // CHUNK <= qchunk, key,
                                                                         jnp.int32(INT_MIN))
        kin = jnp.concatenate([kin_ref[s], jnp.zeros((LANE - TQ, IDX_DIM), BF)], axis=0)
        lane = lax.broadcasted_iota(jnp.int32, (1, LANE), 1)
        key = _score_keys(qst, wst, kin, TQ)
        adm = ((LC + lane) // CHUNK <= qchunk) & (lane < TQ)
        kscr[s * TQ:(s + 1) * TQ, LC:width] = jnp.where(adm, key, jnp.int32(INT_MIN))

    def chunk_pos(c):
        return c * LANE + lax.broadcasted_iota(jnp.int32, (1, LANE), 1)

    def count(pred):
        acc = jnp.zeros((rows, LANE), F32)
        for c in range(width // LANE):
            acc = acc + jnp.where(pred(kscr[:, c * LANE:(c + 1) * LANE], chunk_pos(c)), 1.0, 0.0)
        return jnp.sum(acc, axis=-1, keepdims=True)

    def rewrite(fn):
        for c in range(width // LANE):
            kscr[:, c * LANE:(c + 1) * LANE] = fn(kscr[:, c * LANE:(c + 1) * LANE], chunk_pos(c))

    thr = _topk_threshold(count, rewrite, rows, n_top, width.bit_length())
    thr_ref[...] = thr.reshape(SB, TQ, 1)
    keys_ref[...] = kscr[...].reshape(SB, TQ, width)


def _dec_core_kernel(q_ref, keys_ref, thr_ref, ck_ref, cv_ref, kn_ref, vn_ref, o_ref, *, TQ, LC):
    G = HEAD_GROUP
    q = q_ref[0]
    sel = keys_ref[0] >= thr_ref[0]
    sel_c, sel_n = sel[None, :, 0:LC], sel[None, :, LC:LC + LANE]
    kn_all, vn_all = kn_ref[0], vn_ref[0]
    pad_rows = jnp.zeros((LANE - TQ, HEAD_DIM), BF)
    nt_dims = (((1,), (1,)), ((), ()))
    for g in range(KV_HEADS):
        hs = slice(g * HEAD_DIM, (g + 1) * HEAD_DIM)
        qg = jnp.concatenate([q[:, (G * g + r) * HEAD_DIM:(G * g + r + 1) * HEAD_DIM] for r in range(G)], axis=0)
        kc = ck_ref[0, pl.ds(g, LC, stride=KV_HEADS), :].astype(BF)
        vc = cv_ref[0, pl.ds(g, LC, stride=KV_HEADS), :].astype(BF)
        kn = jnp.concatenate([kn_all[:, hs], pad_rows], axis=0)
        vn = jnp.concatenate([vn_all[:, hs], pad_rows], axis=0)
        sc = lax.dot_general(qg, kc, nt_dims, preferred_element_type=F32).reshape(G, TQ, LC)
        sn = lax.dot_general(qg, kn, nt_dims, preferred_element_type=F32).reshape(G, TQ, LANE)
        sc = jnp.where(sel_c, sc, NEG_BIG).reshape(G * TQ, LC)
        sn = jnp.where(sel_n, sn, NEG_BIG).reshape(G * TQ, LANE)
        m = jnp.maximum(jnp.max(sc, axis=-1, keepdims=True), jnp.max(sn, axis=-1, keepdims=True))
        pc = jnp.exp2(sc - m)
        pn = jnp.exp2(sn - m)
        l = jnp.sum(pc, axis=-1, keepdims=True) + jnp.sum(pn, axis=-1, keepdims=True)
        og = (jnp.dot(pc.astype(BF), vc, preferred_element_type=F32)
              + jnp.dot(pn.astype(BF), vn, preferred_element_type=F32)) / l
        for r in range(G):
            o_ref[0, :, (G * g + r) * HEAD_DIM:(G * g + r + 1) * HEAD_DIM] = og[r * TQ:(r + 1) * TQ].astype(BF)


def _dec_attn_call(q, qi, wi, kib, kb, vb, cache_k, cache_v, cache_kidx, pos0):
    NS, TQ, _ = q.shape
    LC = cache_kidx.shape[1]
    KT, SB = 512, 8
    assert NS % SB == 0 and LC % KT == 0 and TQ <= LANE
    n_top = min(TOPK, (LC + TQ) // 4)
    width = LC + LANE
    blk = lambda n, w: pl.BlockSpec((n, TQ, w), lambda b: (b, 0, 0))
    keys, thr = pl.pallas_call(
        functools.partial(_dec_index_kernel, SB=SB, TQ=TQ, LC=LC, KT=KT, pos0=pos0, n_top=n_top),
        grid=(NS // SB,),
        in_specs=[blk(SB, IDXQ_W), blk(SB, IDX_HEADS),
                  pl.BlockSpec((SB, LC, IDX_DIM), lambda b: (b, 0, 0)), blk(SB, IDX_DIM)],
        out_specs=[blk(SB, width), blk(SB, 1)],
        out_shape=[jax.ShapeDtypeStruct((NS, TQ, width), jnp.int32),
                   jax.ShapeDtypeStruct((NS, TQ, 1), jnp.int32)],
        scratch_shapes=[pltpu.VMEM((SB * TQ, width), jnp.int32)],
        compiler_params=_cparams(("arbitrary",), 48),
        name="attn_dec_index",
    )(qi, wi, cache_kidx, kib)
    cache_spec = pl.BlockSpec((1, LC * KV_HEADS, HEAD_DIM), lambda b: (b, 0, 0))
    return pl.pallas_call(
        functools.partial(_dec_core_kernel, TQ=TQ, LC=LC),
        grid=(NS,),
        in_specs=[blk(1, Q_W), blk(1, width), blk(1, 1), cache_spec, cache_spec, blk(1, KV_W), blk(1, KV_W)],
        out_specs=blk(1, Q_W),
        out_shape=jax.ShapeDtypeStruct((NS, TQ, Q_W), BF),
        compiler_params=_cparams(("arbitrary",), 48),
        name="attn_dec_core",
    )(q, keys, thr, cache_k, cache_v, kb, vb)


def _trunk(x, mods, S, R, TF, TN, W, pos0, past):
    NB, TT, _ = x.shape
    fresh = past is None
    states = {}
    for i in range(DEPTH):
        kind = i % 4
        mod = mods[i]
        gn = W["g_norm"][i]
        x = _ffn_call(x, mod, gn[0:1], W["wg"], W["wu"], W["wd"], i, 0, 0, S, R, TF)
        if kind == 0:
            prefix = None if fresh else jnp.pad(past["pool"], ((0, 0), (1, 0), (0, 0)))
            x, st = _pool_call(x, mod, gn[1:2], W["w_pool"], W["s_pool"], R if fresh else TT, prefix)
            states["pool"] = st[:, 1:]
        elif kind == 1:
            prefix = None if fresh else jnp.pad(past["sconv"], ((0, 0), (SCONV_PAD - 2, 0), (0, 0)))
            z, st = _conv_call(x, mod, gn[1:2], W["w_sc_in"], "sconv", S, R, TN, W["w_sc_conv"],
                               prefix=prefix)
            states["sconv"] = st
            x = _linres_call(z, x, mod, 5, W["w_sc_out"], S, R)
        elif kind == 2:
            reps = 1 if fresh else NB
            tabs = _rope_tables(pos0 + jnp.arange(TT), reps)
            q, k32, kb, v32, vb, qi, ki32, kib, wi = _aproj_call(
                x, mod, gn[1:2], W["w_attn_main"], W["w_attn_tail"], tabs, S, R)
            states["k"], states["v"], states["kidx"] = k32, v32, ki32
            if fresh:
                o = _acore_call(q, qi, wi, kib, kb, vb, 128, 512, pos0, TT)
            else:
                o = _dec_attn_call(q, qi, wi, kib, kb, vb, past["k"], past["v"], past["kidx"], pos0)
            x = _linres_call(o, x, mod, 5, W["w_attn_out"], S, R)
        else:
            prefix = None if fresh else jnp.pad(past["cconv"], ((0, 0), (CCONV_PAD - 30, 0), (0, 0)))
            cv, st = _conv_call(x, mod, gn[1:2], W["w_cm_pw1"], "cconv", S, R, TN, W["w_cm_dw"],
                                pbias=W["b_cm_pw1"], b_dw=W["b_cm_dw"], prefix=prefix)
            states["cconv"] = st
            x = _linres_call(cv, x, mod, 5, W["w_cm_pw2"], S, R,
                             ln=(W["g_cm_ln"], W["b_cm_ln"]), bias=W["b_cm_pw2"])
        gfin = W["g_final"] if i == DEPTH - 1 else None
        x = _ffn_call(x, mod, gn[2:3], W["wg"], W["wu"], W["wd"], i, 1, 2, S, R, TF, gfin)
    return x, states


def kernel(x_prompt, x_sample, state_pool, state_sconv, cache_k, cache_v, cache_kidx, state_cconv, c_prompt, c_sample, w_mod, b_mod, g_norm, w_ffn_gate, w_ffn_up, w_ffn_down, w_pool, s_pool, w_sc_in, w_sc_conv, w_sc_out, w_attn_in, w_attn_out, w_cm_pw1, b_cm_pw1, w_cm_dw, b_cm_dw, g_cm_ln, b_cm_ln, w_cm_pw2, b_cm_pw2, g_final):
    B, T, _ = x_prompt.shape
    NS, TS, _ = x_sample.shape
    assert D_FF % 512 == 0 and T % 512 == 0 and B + NS <= MOD_ROWS
    assert w_pool.shape[0] == w_sc_in.shape[0] == w_attn_in.shape[0] == w_cm_pw1.shape[0] == 1

    c_all = jnp.concatenate([c_prompt, c_sample, jnp.zeros((MOD_ROWS - B - NS, D), F32)], axis=0)
    mod_all = _mod_call(c_all, w_mod, b_mod)
    mods_p = [mod_all[i, 0:B].reshape(B, 1, 9 * D) for i in range(DEPTH)]
    mods_s = [mod_all[i, B:B + NS].reshape(NS, 1, 9 * D) for i in range(DEPTH)]

    wa = w_attn_in[0].astype(BF)
    W = dict(
        g_norm=g_norm, g_final=g_final.reshape(1, D),
        wg=w_ffn_gate.astype(BF), wu=w_ffn_up.astype(BF), wd=w_ffn_down.astype(BF),
        w_pool=w_pool[0].astype(BF), s_pool=s_pool,
        w_sc_in=w_sc_in[0].astype(BF), w_sc_conv=w_sc_conv[0], w_sc_out=w_sc_out[0].astype(BF),
        w_attn_main=wa, w_attn_tail=jnp.pad(wa[:, ATTN_MAIN:], ((0, 0), (0, LANE - ATTN_TAIL))),
        w_attn_out=w_attn_out[0].astype(BF),
        w_cm_pw1=w_cm_pw1[0].astype(BF), b_cm_pw1=b_cm_pw1, w_cm_dw=w_cm_dw[0], b_cm_dw=b_cm_dw,
        g_cm_ln=g_cm_ln, b_cm_ln=b_cm_ln, w_cm_pw2=w_cm_pw2[0].astype(BF), b_cm_pw2=b_cm_pw2,
    )

    y_p, st_p = _trunk(x_prompt, mods_p, 1, 512, 512, 512, W, 0, None)
    past = dict(pool=state_pool[0], sconv=state_sconv[0], cconv=state_cconv[0],
                k=cache_k[0].reshape(NS, -1, HEAD_DIM), v=cache_v[0].reshape(NS, -1, HEAD_DIM),
                kidx=cache_kidx[0])
    y_s, st_s = _trunk(x_sample, mods_s, NS, TS, 512, 512, W, cache_k.shape[2], past)

    def kv(a):
        return a.reshape(1, a.shape[0], a.shape[1], KV_HEADS, HEAD_DIM)

    return (y_p, y_s, st_p["pool"][None], st_s["pool"][None], st_p["sconv"][None], st_s["sconv"][None],
            kv(st_p["k"]), kv(st_s["k"]), kv(st_p["v"]), kv(st_s["v"]),
            st_p["kidx"][None], st_s["kidx"][None], st_p["cconv"][None], st_s["cconv"][None])
```

```python
import functools
import math

import jax
import jax.numpy as jnp
import numpy as np
from jax import lax
from jax.experimental import pallas as pl
from jax.experimental.pallas import tpu as pltpu

D = 2048
D_FF = 5632
DEPTH = 4
NORM_EPS = 1e-6
CHUNK = 64
POOL_WINDOWS = (2, 4, 8, 16)
POOL_GROUP = D // 4
POOL_PAD = 16
SCONV_WIDTH = 3
SCONV_PAD = 8
CCONV_WIDTH = 31
CCONV_PAD = 32
N_HEADS = 16
HEAD_DIM = 128
KV_HEADS = 4
HEAD_GROUP = N_HEADS // KV_HEADS
ROPE_DIM = HEAD_DIM // 4
ROPE_THETA = 500000.0
IDX_HEADS = 16
IDX_DIM = 64
IDX_ROPE_DIM = IDX_DIM // 4
IDX_SCALE = IDX_HEADS ** -0.5 * IDX_DIM ** -0.5
TOPK = 256
Q_W = N_HEADS * HEAD_DIM
KV_W = KV_HEADS * HEAD_DIM
IDXQ_W = IDX_HEADS * IDX_DIM
ATTN_MAIN = Q_W + 2 * KV_W + IDXQ_W
ATTN_TAIL = IDX_DIM + IDX_HEADS
LANE = 128
MOD_ROWS = 40
Q_SCALE = HEAD_DIM ** -0.5 * math.log2(math.e)
INT_MIN = -2 ** 31
NEG_BIG = -1e30

BF = jnp.bfloat16
F32 = jnp.float32


BF16_ROWS = 16
SUBLANES = 8
MXU_WIDTH = 256
CONV_CHUNK_ROWS = 64
CONV_CHUNK_LANES = 256


def _round_up(n, m):
    return -(-n // m) * m


def _cparams(sem, vmem_mib):
    return pltpu.CompilerParams(dimension_semantics=sem, vmem_limit_bytes=vmem_mib << 20)


def _rms(x, g):
    return x * lax.rsqrt(jnp.mean(x * x, axis=-1, keepdims=True) + NORM_EPS) * g


def _rms_mod(x, g, shift, scale):
    return _rms(x, g) * (1.0 + scale) + shift


def _silu(x):
    return x * jax.nn.sigmoid(x)


def _row_spec(S, R, W):
    return pl.BlockSpec((S, R, W), lambda b, t, n: (b, t, 0))


def _mod_spec(S, col):
    return pl.BlockSpec((S, 1, D), lambda b, t, n: (b, 0, col))


def _const_spec(shape):
    nd = len(shape)
    return pl.BlockSpec(shape, lambda b, t, n: (0,) * nd)


def _mod_kernel(c_ref, w_ref, b_ref, o_ref):
    a = _silu(c_ref[...]).astype(BF)
    o_ref[0] = jnp.dot(a, w_ref[0].astype(BF), preferred_element_type=F32) + b_ref[0]


def _mod_call(c_all, w_mod, b_mod):
    TN = 1024
    N = 9 * D
    return pl.pallas_call(
        _mod_kernel,
        grid=(DEPTH, N // TN),
        in_specs=[pl.BlockSpec((MOD_ROWS, D), lambda i, n: (0, 0)),
                  pl.BlockSpec((1, D, TN), lambda i, n: (i, 0, n)),
                  pl.BlockSpec((1, 1, TN), lambda i, n: (i, 0, n))],
        out_specs=pl.BlockSpec((1, MOD_ROWS, TN), lambda i, n: (i, 0, n)),
        out_shape=jax.ShapeDtypeStruct((DEPTH, MOD_ROWS, N), F32),
        compiler_params=_cparams(("arbitrary", "arbitrary"), 40),
        name="mod",
    )(c_all, w_mod, b_mod.reshape(DEPTH, 1, N))


def _ffn_kernel(*refs, S, R, final, cast_next):
    refs = list(refs)
    x_ref, sh_ref, sc_ref, gt_ref, g_ref, wg_ref, wu_ref, wd_ref = refs[:8]
    rest = refs[8:]
    if final:
        gf_ref = rest[0]
        rest = rest[1:]
    if cast_next:
        next32 = rest[:3]
        o_ref, *next16, h_scr = rest[3:]
    else:
        o_ref, h_scr = rest
    f = pl.program_id(2)
    M = S * R

    @pl.when(f == 0)
    def _():
        h = _rms_mod(x_ref[...], g_ref[...], sh_ref[...], sc_ref[...])
        h_scr[...] = h.reshape(M, D).astype(BF)
        o_ref[...] = jnp.zeros_like(o_ref)

    h = h_scr[...]
    TF = wg_ref.shape[1]
    acc = None
    for c0 in range(0, TF, MXU_WIDTH):
        cs = slice(c0, c0 + MXU_WIDTH)
        g = jnp.dot(h, wg_ref[:, cs], preferred_element_type=F32)
        u = jnp.dot(h, wu_ref[:, cs], preferred_element_type=F32)
        a = (_silu(g) * u).astype(BF)
        part = jnp.dot(a, wd_ref[cs, :], preferred_element_type=F32)
        acc = part if acc is None else acc + part
    o_ref[...] += acc.reshape(S, R, D)

    if cast_next:
        for src, dst in zip(next32, next16):
            dst[...] = src[...].astype(BF)

    @pl.when(f == pl.num_programs(2) - 1)
    def _():
        y = x_ref[...] + 0.5 * gt_ref[...] * o_ref[...]
        if final:
            y = _rms(y, gf_ref[...])
        o_ref[...] = y


def _ffn_call(x, mod, g, w16, sub, S, R, TF, gfin=None, next32=None):
    NB, TT, _ = x.shape
    final = gfin is not None
    nb, nt, nf = NB // S, TT // R, D_FF // TF
    in_specs = [_row_spec(S, R, D), _mod_spec(S, 3 * sub), _mod_spec(S, 3 * sub + 1),
                _mod_spec(S, 3 * sub + 2), _const_spec((1, D)),
                pl.BlockSpec((D, TF), lambda b, t, f: (0, f)),
                pl.BlockSpec((D, TF), lambda b, t, f: (0, f)),
                pl.BlockSpec((TF, D), lambda b, t, f: (f, 0))]
    args = [x, mod, mod, mod, g, *w16]
    if final:
        in_specs.append(_const_spec((1, D)))
        args.append(gfin)
    out_specs = [_row_spec(S, R, D)]
    out_shape = [jax.ShapeDtypeStruct(x.shape, F32)]
    if next32 is not None:
        (wg32, wu32, wd32), layer, which = next32
        up_rows, down_rows = D // (nb * nt), D_FF // (nb * nt * nf)
        assert up_rows * nb * nt == D and down_rows * nb * nt * nf == D_FF
        assert up_rows % BF16_ROWS == 0 and down_rows % BF16_ROWS == 0
        up_idx = lambda b, t, f: (b * nt + t, f)
        down_idx = lambda b, t, f: ((b * nt + t) * nf + f, 0)
        in_specs += [pl.BlockSpec((None, None, up_rows, TF), lambda b, t, f: (layer, which, *up_idx(b, t, f))),
                     pl.BlockSpec((None, None, up_rows, TF), lambda b, t, f: (layer, which, *up_idx(b, t, f))),
                     pl.BlockSpec((None, None, down_rows, D), lambda b, t, f: (layer, which, *down_idx(b, t, f)))]
        args += [wg32, wu32, wd32]
        out_specs += [pl.BlockSpec((up_rows, TF), up_idx), pl.BlockSpec((up_rows, TF), up_idx),
                      pl.BlockSpec((down_rows, D), down_idx)]
        out_shape += [jax.ShapeDtypeStruct((D, D_FF), BF), jax.ShapeDtypeStruct((D, D_FF), BF),
                      jax.ShapeDtypeStruct((D_FF, D), BF)]
    y, *w16_next = pl.pallas_call(
        functools.partial(_ffn_kernel, S=S, R=R, final=final, cast_next=next32 is not None),
        grid=(nb, nt, nf),
        in_specs=in_specs,
        out_specs=out_specs,
        out_shape=out_shape,
        scratch_shapes=[pltpu.VMEM((S * R, D), BF)],
        compiler_params=_cparams(("arbitrary", "arbitrary", "arbitrary"), 48),
        name="ffn",
    )(*args)
    return y, tuple(w16_next)


def _linres_kernel(*refs, S, R, ln, bias):
    refs = list(refs)
    a_ref, x_ref, gt_ref, w_ref = refs[:4]
    rest = refs[4:]
    if ln:
        gl_ref, bl_ref = rest[:2]
        rest = rest[2:]
    if bias:
        b_ref = rest[0]
        rest = rest[1:]
    (o_ref,) = rest
    a = a_ref[...]
    if ln:
        a = a.astype(F32)
        mu = jnp.mean(a, axis=-1, keepdims=True)
        ac = a - mu
        a = ac * lax.rsqrt(jnp.mean(ac * ac, axis=-1, keepdims=True) + NORM_EPS)
        a = _silu(a * gl_ref[...] + bl_ref[...])
    a = a.reshape(S * R, a.shape[-1]).astype(BF)
    y = jnp.dot(a, w_ref[...], preferred_element_type=F32)
    if bias:
        y = y + b_ref[...]
    o_ref[...] = x_ref[...] + gt_ref[...] * y.reshape(o_ref.shape)


def _linres_call(a, x, mod, gate_col, w, S, R, ln=None, bias=None):
    NB, TT, K = a.shape
    in_specs = [_row_spec(S, R, K), _row_spec(S, R, D), _mod_spec(S, gate_col),
                pl.BlockSpec((K, D), lambda b, t, n: (0, 0), pipeline_mode=pl.Buffered(1))]
    args = [a, x, mod, w]
    if ln is not None:
        in_specs += [_const_spec((1, K)), _const_spec((1, K))]
        args += list(ln)
    if bias is not None:
        in_specs.append(_const_spec((1, D)))
        args.append(bias)
    return pl.pallas_call(
        functools.partial(_linres_kernel, S=S, R=R, ln=ln is not None, bias=bias is not None),
        grid=(NB // S, TT // R, 1),
        in_specs=in_specs,
        out_specs=_row_spec(S, R, D),
        out_shape=jax.ShapeDtypeStruct(x.shape, F32),
        compiler_params=_cparams(("arbitrary", "arbitrary", "arbitrary"), 56),
        name="linres",
    )(*args)


def _pool_kernel(*refs, R, has_prefix, n_valid_prefix):
    if has_prefix:
        (x_ref, sh_ref, sc_ref, gt_ref, g_ref, wp_ref, sp_ref, pre_ref,
         o_ref, st_ref, hext) = refs
    else:
        (x_ref, sh_ref, sc_ref, gt_ref, g_ref, wp_ref, sp_ref,
         o_ref, st_ref, hext, carry) = refs
    t = pl.program_id(1)
    x = x_ref[0]
    h = _rms_mod(x, g_ref[...], sh_ref[0], sc_ref[0])
    if has_prefix:
        hext[0:POOL_PAD, :] = pre_ref[0]
    else:
        @pl.when(t == 0)
        def _():
            hext[0:POOL_PAD, :] = jnp.zeros((POOL_PAD, D), F32)

        @pl.when(t > 0)
        def _():
            hext[0:POOL_PAD, :] = carry[...]

        carry[...] = h[R - POOL_PAD:R, :]
    hext[POOL_PAD:POOL_PAD + R, :] = h
    st_ref[0] = h[R - POOL_PAD:R, :]
    tpos = t * R + lax.broadcasted_iota(jnp.int32, (R, 1), 0)
    gate = gt_ref[0]
    for gi, w in enumerate(POOL_WINDOWS):
        lo = gi * POOL_GROUP
        hi = lo + POOL_GROUP
        rc = min(R, CONV_CHUNK_ROWS)
        wins = []
        for r0 in range(0, R, rc):
            acc = hext[POOL_PAD + r0:POOL_PAD + r0 + rc, lo:hi]
            for j in range(1, w):
                acc = acc + hext[POOL_PAD + r0 - j:POOL_PAD + r0 - j + rc, lo:hi]
            wins.append(acc)
        win = jnp.concatenate(wins, axis=0) if len(wins) > 1 else wins[0]
        cnt = jnp.minimum(tpos + 1 + n_valid_prefix, w).astype(F32)
        pooled = (win / cnt - h[:, lo:hi]).astype(BF)
        y = jnp.dot(pooled, wp_ref[gi], preferred_element_type=F32) * sp_ref[:, lo:hi]
        o_ref[0, :, lo:hi] = x[:, lo:hi] + gate[:, lo:hi] * y


def _pool_call(x, mod, g, w_pool, s_pool, R, prefix=None):
    NB, TT, _ = x.shape
    has_prefix = prefix is not None
    in_specs = [_row_spec(1, R, D), _mod_spec(1, 3), _mod_spec(1, 4), _mod_spec(1, 5),
                _const_spec((1, D)), _const_spec((4, POOL_GROUP, POOL_GROUP)), _const_spec((1, D))]
    args = [x, mod, mod, mod, g, w_pool, s_pool]
    scratch = [pltpu.VMEM((R + POOL_PAD, D), F32)]
    if has_prefix:
        in_specs.append(pl.BlockSpec((1, POOL_PAD, D), lambda b, t, n: (b, 0, 0)))
        args.append(prefix)
    else:
        scratch.append(pltpu.VMEM((POOL_PAD, D), F32))
    return pl.pallas_call(
        functools.partial(_pool_kernel, R=R, has_prefix=has_prefix,
                          n_valid_prefix=POOL_PAD - 1 if has_prefix else 0),
        grid=(NB, TT // R, 1),
        in_specs=in_specs,
        out_specs=[_row_spec(1, R, D), pl.BlockSpec((1, POOL_PAD, D), lambda b, t, n: (b, 0, 0))],
        out_shape=[jax.ShapeDtypeStruct(x.shape, F32), jax.ShapeDtypeStruct((NB, POOL_PAD, D), F32)],
        scratch_shapes=scratch,
        compiler_params=_cparams(("arbitrary", "arbitrary", "arbitrary"), 40),
        name="pool",
    )(*args)


def _conv_kernel(*refs, S, R, TN, mode, has_prefix):
    refs = list(refs)
    x_ref, sh_ref, sc_ref, g_ref = refs[:4]
    rest = refs[4:]
    nproj = 3 if mode == "sconv" else 2
    w_refs = rest[:nproj]
    rest = rest[nproj:]
    if mode == "cconv":
        pb_refs = rest[:2]
        wc_ref, bdw_ref = rest[2:4]
        rest = rest[4:]
        width, pad = CCONV_WIDTH, CCONV_PAD
    else:
        wc_ref = rest[0]
        rest = rest[1:]
        width, pad = SCONV_WIDTH, SCONV_PAD
    if has_prefix:
        pre_ref = rest[0]
        z_ref, st_ref, h_scr, uext, vbuf = rest[1:]
    else:
        z_ref, st_ref, h_scr, uext, vbuf, carry = rest
    t = pl.program_id(1)
    n = pl.program_id(2)
    M = S * R

    @pl.when(n == 0)
    def _():
        h = _rms_mod(x_ref[...], g_ref[...], sh_ref[...], sc_ref[...])
        h_scr[...] = h.reshape(M, D).astype(BF)

    h = h_scr[...]
    proj = [jnp.dot(h, w_ref[...], preferred_element_type=F32) for w_ref in w_refs]
    if mode == "sconv":
        bq, cq, vq = proj
        u = cq * vq
    else:
        a = proj[0] + pb_refs[0][...]
        gq = proj[1] + pb_refs[1][...]
        u = a * jax.nn.sigmoid(gq)
    u3 = u.reshape(S, R, TN)
    if has_prefix:
        uext[:, 0:pad, :] = pre_ref[...]
    else:
        @pl.when(t == 0)
        def _():
            uext[:, 0:pad, :] = jnp.zeros((S, pad, TN), F32)

        @pl.when(t > 0)
        def _():
            uext[:, 0:pad, :] = carry[n]
    uext[:, pad:pad + R, :] = u3
    tail = uext[:, R:R + pad, :]
    if not has_prefix:
        carry[n] = tail
    st_ref[...] = tail
    wc = wc_ref[...]
    if mode == "sconv":
        bq3 = bq.reshape(S, R, TN)
    base = pad - (width - 1)
    rc = min(R, CONV_CHUNK_ROWS)
    sc_ = max(1, CONV_CHUNK_ROWS // rc)
    for s0 in range(0, S, sc_):
        for r0 in range(0, R, rc):
            for l0 in range(0, TN, CONV_CHUNK_LANES):
                ls = slice(l0, l0 + CONV_CHUNK_LANES)
                acc = None
                for sub in range(min(SUBLANES, width)):
                    taps = range(sub, width, SUBLANES)
                    start = r0 + base + sub
                    span = rc + taps[-1] - sub
                    vbuf[:, 0:span, :] = uext[s0:s0 + sc_, start:start + span, ls]
                    for j in taps:
                        term = wc[j:j + 1, ls] * vbuf[:, j - sub:j - sub + rc, :]
                        acc = term if acc is None else acc + term
                if mode == "sconv":
                    z_ref[s0:s0 + sc_, r0:r0 + rc, ls] = (bq3[s0:s0 + sc_, r0:r0 + rc, ls] * acc).astype(z_ref.dtype)
                else:
                    z_ref[s0:s0 + sc_, r0:r0 + rc, ls] = acc + bdw_ref[:, ls]


def _conv_call(x, mod, g, w, mode, S, R, TN, w_conv, pbias=None, b_dw=None, prefix=None):
    NB, TT, _ = x.shape
    has_prefix = prefix is not None
    nproj = 3 if mode == "sconv" else 2
    width, pad = (SCONV_WIDTH, SCONV_PAD) if mode == "sconv" else (CCONV_WIDTH, CCONV_PAD)
    nN = D // TN
    in_specs = [_row_spec(S, R, D), _mod_spec(S, 3), _mod_spec(S, 4), _const_spec((1, D))]
    args = [x, mod, mod, g]
    for k in range(nproj):
        in_specs.append(pl.BlockSpec((D, TN), lambda b, t, n, k=k: (0, k * nN + n)))
        args.append(w)
    if mode == "cconv":
        for k in range(2):
            in_specs.append(pl.BlockSpec((1, TN), lambda b, t, n, k=k: (0, k * nN + n)))
            args.append(pbias)
    in_specs.append(pl.BlockSpec((width, TN), lambda b, t, n: (0, n)))
    args.append(w_conv)
    if mode == "cconv":
        in_specs.append(pl.BlockSpec((1, TN), lambda b, t, n: (0, n)))
        args.append(b_dw)
    rc = min(R, CONV_CHUNK_ROWS)
    scratch = [pltpu.VMEM((S * R, D), BF), pltpu.VMEM((S, R + pad, TN), F32),
               pltpu.VMEM((max(1, CONV_CHUNK_ROWS // rc), rc + _round_up(width, SUBLANES), CONV_CHUNK_LANES), F32)]
    if has_prefix:
        in_specs.append(pl.BlockSpec((S, pad, TN), lambda b, t, n: (b, 0, n)))
        args.append(prefix)
    else:
        scratch.append(pltpu.VMEM((nN, S, pad, TN), F32))
    zdt = BF if mode == "sconv" else F32
    z, tails = pl.pallas_call(
        functools.partial(_conv_kernel, S=S, R=R, TN=TN, mode=mode, has_prefix=has_prefix),
        grid=(NB // S, TT // R, nN),
        in_specs=in_specs,
        out_specs=[pl.BlockSpec((S, R, TN), lambda b, t, n: (b, t, n)),
                   pl.BlockSpec((S, pad, TN), lambda b, t, n: (b, t, n))],
        out_shape=[jax.ShapeDtypeStruct(x.shape, zdt),
                   jax.ShapeDtypeStruct((NB, (TT // R) * pad, D), F32)],
        scratch_shapes=scratch,
        compiler_params=_cparams(("arbitrary", "arbitrary", "arbitrary"), 40),
        name=mode,
    )(*args)
    return z, tails[:, -(width - 1):]


def _rope_lanes(x, cos, sin, period, half):
    lane = lax.broadcasted_iota(jnp.int32, x.shape, x.ndim - 1) % period
    width = x.shape[-1]
    partner = jnp.where(lane < half, pltpu.roll(x, width - half, x.ndim - 1),
                        pltpu.roll(x, half, x.ndim - 1))
    return x * cos + partner * sin


def _aproj_kernel(x_ref, sh_ref, sc_ref, g_ref, w_ref, wt_ref,
                  ch_ref, sh_h_ref, ci_ref, si_ref, ct_ref, st_ref,
                  q_ref, k32_ref, kb_ref, v32_ref, vb_ref, qi_ref, ki32_ref, kib_ref, wi_ref,
                  h_scr, *, S, R):
    n = pl.program_id(2)
    M = S * R
    TN = 4 * HEAD_DIM

    @pl.when(n == 0)
    def _():
        h = _rms_mod(x_ref[...], g_ref[...], sh_ref[...], sc_ref[...])
        h_scr[...] = h.reshape(M, D).astype(BF)

    def heads_rope(p):
        cos, sin = ch_ref[...], sh_h_ref[...]
        return jnp.concatenate(
            [_rope_lanes(p[:, i * HEAD_DIM:(i + 1) * HEAD_DIM], cos, sin, HEAD_DIM, ROPE_DIM // 2)
             for i in range(TN // HEAD_DIM)], axis=-1)

    @pl.when(n < 4)
    def _():
        p = jnp.dot(h_scr[...], w_ref[...], preferred_element_type=F32)
        q = heads_rope(p) * Q_SCALE
        q_ref[...] = q.reshape(S, R, TN).astype(BF)

    @pl.when(n == 4)
    def _():
        p = jnp.dot(h_scr[...], w_ref[...], preferred_element_type=F32)
        k = heads_rope(p).reshape(S, R, TN)
        k32_ref[...] = k
        kb_ref[...] = k.astype(BF)

    @pl.when(n == 5)
    def _():
        v = jnp.dot(h_scr[...], w_ref[...], preferred_element_type=F32).reshape(S, R, TN)
        v32_ref[...] = v
        vb_ref[...] = v.astype(BF)

    @pl.when((n == 6) | (n == 7))
    def _():
        p = jnp.dot(h_scr[...], w_ref[...], preferred_element_type=F32)
        cos, sin = ci_ref[...], si_ref[...]
        qi = jnp.concatenate(
            [_rope_lanes(p[:, i * LANE:(i + 1) * LANE], cos, sin, IDX_DIM, IDX_ROPE_DIM // 2)
             for i in range(TN // LANE)], axis=-1)
        qi_ref[...] = qi.reshape(S, R, TN).astype(BF)

    @pl.when(n == 8)
    def _():
        p = jnp.dot(h_scr[...], wt_ref[...], preferred_element_type=F32)
        tl = _rope_lanes(p, ct_ref[...], st_ref[...], LANE, IDX_ROPE_DIM // 2)
        ki = tl[:, 0:IDX_DIM].reshape(S, R, IDX_DIM)
        ki32_ref[...] = ki
        kib_ref[...] = ki.astype(BF)
        wi_ref[...] = (tl[:, IDX_DIM:IDX_DIM + IDX_HEADS] * IDX_SCALE).reshape(S, R, IDX_HEADS)


def _aproj_call(x, mod, g, w_main, w_tail, tabs, S, R):
    NB, TT, _ = x.shape
    M = S * R
    TN = 4 * HEAD_DIM
    tab_spec = pl.BlockSpec((M, LANE), lambda b, t, n: (t, 0))

    def out(width, dt, idx=lambda b, t, n: (b, t, 0)):
        return (pl.BlockSpec((S, R, min(width, TN)), idx), jax.ShapeDtypeStruct((NB, TT, width), dt))

    outs = [out(Q_W, BF, lambda b, t, n: (b, t, jnp.minimum(n, 3))),
            out(KV_W, F32), out(KV_W, BF), out(KV_W, F32), out(KV_W, BF),
            out(IDXQ_W, BF, lambda b, t, n: (b, t, jnp.clip(n - 6, 0, 1))),
            out(IDX_DIM, F32), out(IDX_DIM, BF), out(IDX_HEADS, F32)]
    return pl.pallas_call(
        functools.partial(_aproj_kernel, S=S, R=R),
        grid=(NB // S, TT // R, 9),
        in_specs=[_row_spec(S, R, D), _mod_spec(S, 3), _mod_spec(S, 4), _const_spec((1, D)),
                  pl.BlockSpec((D, TN), lambda b, t, n: (0, jnp.minimum(n, ATTN_MAIN // TN - 1))),
                  _const_spec((D, LANE))] + [tab_spec] * 6,
        out_specs=[o[0] for o in outs],
        out_shape=[o[1] for o in outs],
        scratch_shapes=[pltpu.VMEM((M, D), BF)],
        compiler_params=_cparams(("arbitrary", "arbitrary", "arbitrary"), 40),
        name="attn_proj",
    )(x, mod, mod, g, w_main, w_tail, *tabs)


def _rope_tables(pos, reps):
    def tab(rot_dim, period, lanes_used):
        half = rot_dim // 2
        inv = jnp.exp(-math.log(ROPE_THETA) * jnp.arange(half, dtype=F32) * (2.0 / rot_dim))
        ang = pos.astype(F32)[:, None] * inv[None, :]
        cos, sin = jnp.cos(ang), jnp.sin(ang)
        n = pos.shape[0]
        one = jnp.ones((n, period - rot_dim), F32)
        zero = jnp.zeros((n, period - rot_dim), F32)
        c = jnp.concatenate([cos, cos, one], axis=1)
        s = jnp.concatenate([-sin, sin, zero], axis=1)
        c = jnp.tile(c, (1, lanes_used // period))
        s = jnp.tile(s, (1, lanes_used // period))
        if lanes_used < LANE:
            c = jnp.concatenate([c, jnp.ones((n, LANE - lanes_used), F32)], axis=1)
            s = jnp.concatenate([s, jnp.zeros((n, LANE - lanes_used), F32)], axis=1)
        return [jnp.tile(c, (reps, 1)), jnp.tile(s, (reps, 1))]

    return tab(ROPE_DIM, HEAD_DIM, LANE) + tab(IDX_ROPE_DIM, IDX_DIM, LANE) + tab(IDX_ROPE_DIM, IDX_DIM, IDX_DIM)


def _topk_threshold(count, rewrite, rows, n_top, pos_bits, count_hi=None):
    n_top = float(n_top)

    def bit_step(i, carry, nbits, count_ge):
        thr, n_ge = carry
        cand = jnp.where(i == 0, jnp.zeros_like(thr), thr | jnp.left_shift(jnp.int32(1), nbits - 1 - i))
        n_cand = count_ge(cand)
        take = n_cand >= n_top
        return jnp.where(take, cand, thr), jnp.where(take, n_cand, n_ge)

    count32 = lambda cand: count(lambda k, p: k >= cand)
    n_ge = jnp.zeros((rows, 1), F32)
    if count_hi is None:
        first, thr = 0, jnp.full((rows, 1), INT_MIN, jnp.int32)
    else:
        first = 16
        thr, n_ge = lax.fori_loop(0, first, functools.partial(bit_step, nbits=16, count_ge=count_hi),
                                  (jnp.full((rows, 1), -2 ** 15, jnp.int32), n_ge))
        thr = jnp.left_shift(thr, 16)
    thr, n_ge = lax.fori_loop(first, 32, functools.partial(bit_step, nbits=32, count_ge=count32), (thr, n_ge))
    excess = n_ge > n_top

    @pl.when(jnp.max(jnp.where(excess, 1, 0)) > 0)
    def _():
        quota = n_top - count(lambda k, p: k > thr)

        def pos_body(i, cut):
            cand = cut | jnp.left_shift(jnp.int32(1), pos_bits - 1 - i)
            return jnp.where(count(lambda k, p: (k == thr) & (p < cand)) < quota, cand, cut)

        cut = lax.fori_loop(0, pos_bits, pos_body, jnp.zeros((rows, 1), jnp.int32))
        rewrite(lambda k, p: jnp.where((k == thr) & (p > cut), jnp.int32(INT_MIN), k))

    return jnp.maximum(thr, jnp.int32(INT_MIN + 1))


def _acore_kernel(q_ref, qi_ref, wi_ref, ki_ref, kb_ref, vb_ref, o_ref,
                  keys, keys_hi, m_scr, l_scr, acc_scr, *, TQ, KT, NT, pos0, l_valid, n_top):
    j = pl.program_id(1)
    q0 = pos0 + j * TQ
    last_key = ((q0 + TQ - 1) // CHUNK) * CHUNK + CHUNK - 1
    nt = jnp.minimum(last_key // KT + 1, NT)
    qchunk = (q0 + lax.broadcasted_iota(jnp.int32, (TQ, 1), 0)) // CHUNK

    qi = qi_ref[0]
    qst = jnp.concatenate([qi[:, h * IDX_DIM:(h + 1) * IDX_DIM] for h in range(IDX_HEADS)], axis=0)
    wi = wi_ref[0]
    wst = jnp.concatenate([wi[:, h:h + 1] for h in range(IDX_HEADS)], axis=0)

    def score_body(kt, carry):
        start = pl.multiple_of(kt * KT, KT)
        key = _score_keys(qst, wst, ki_ref[0, pl.ds(start, KT), :], TQ)
        kpos = start + lax.broadcasted_iota(jnp.int32, (1, KT), 1)
        adm = (kpos // CHUNK <= qchunk) & (kpos < l_valid)
        key = jnp.where(adm, key, jnp.int32(INT_MIN))
        keys[kt] = key
        keys_hi[kt] = jnp.right_shift(key, 16).astype(jnp.int16)
        return carry

    lax.fori_loop(0, nt, score_body, 0)

    def count_hi(cand):
        cand16 = cand.astype(jnp.int16)

        def body(kt, acc):
            hit = jnp.where(keys_hi[kt] >= cand16, jnp.int16(1), jnp.int16(0))
            for c in range(KT // LANE):
                acc = acc + hit[:, c * LANE:(c + 1) * LANE]
            return acc
        acc = lax.fori_loop(0, nt, body, jnp.zeros((TQ, LANE), jnp.int16))
        return jnp.sum(acc.astype(F32), axis=-1, keepdims=True)

    def tile_pos(kt):
        return kt * KT + lax.broadcasted_iota(jnp.int32, (1, KT), 1)

    def count(pred):
        def body(kt, acc):
            hit = jnp.where(pred(keys[kt], tile_pos(kt)), 1.0, 0.0)
            for c in range(KT // LANE):
                acc = acc + hit[:, c * LANE:(c + 1) * LANE]
            return acc
        acc = lax.fori_loop(0, nt, body, jnp.zeros((TQ, LANE), F32))
        return jnp.sum(acc, axis=-1, keepdims=True)

    def rewrite(fn):
        def body(kt, carry):
            keys[kt] = fn(keys[kt], tile_pos(kt))
            return carry
        lax.fori_loop(0, nt, body, 0)

    thr = _topk_threshold(count, rewrite, TQ, n_top, (NT * KT).bit_length(), count_hi)

    q = q_ref[0]
    G = HEAD_GROUP
    qgs = [jnp.concatenate([q[:, (G * g + r) * HEAD_DIM:(G * g + r + 1) * HEAD_DIM] for r in range(G)],
                           axis=0) for g in range(KV_HEADS)]

    def scores(kt, g):
        start = pl.multiple_of(kt * KT, KT)
        kt_ = kb_ref[0, pl.ds(start, KT), g * HEAD_DIM:(g + 1) * HEAD_DIM]
        s = lax.dot_general(qgs[g], kt_, (((1,), (1,)), ((), ())), preferred_element_type=F32)
        return s.reshape(G, TQ, KT)

    def lane_fold(x, op):
        part = x[:, 0:LANE]
        for c in range(1, KT // LANE):
            part = op(part, x[:, c * LANE:(c + 1) * LANE])
        return part

    m_scr[...] = jnp.full((KV_HEADS, G * TQ, LANE), NEG_BIG, F32)

    def max_body(kt, carry):
        sel = (keys[kt] >= thr)[None]
        for g in range(KV_HEADS):
            s = jnp.where(sel, scores(kt, g), NEG_BIG).reshape(G * TQ, KT)
            m_scr[g] = jnp.maximum(m_scr[g], lane_fold(s, jnp.maximum))
        return carry

    lax.fori_loop(0, nt, max_body, 0)
    ms = [jnp.max(m_scr[g], axis=-1, keepdims=True).reshape(G, TQ, 1) for g in range(KV_HEADS)]
    l_scr[...] = jnp.zeros((KV_HEADS, G * TQ, LANE), F32)
    acc_scr[...] = jnp.zeros((KV_HEADS, G * TQ, HEAD_DIM), F32)

    def pv_body(kt, carry):
        start = pl.multiple_of(kt * KT, KT)
        sel = (keys[kt] >= thr)[None]
        for g in range(KV_HEADS):
            vt_ = vb_ref[0, pl.ds(start, KT), g * HEAD_DIM:(g + 1) * HEAD_DIM]
            p = jnp.where(sel, jnp.exp2(scores(kt, g) - ms[g]), 0.0).reshape(G * TQ, KT)
            l_scr[g] += lane_fold(p, jnp.add)
            acc_scr[g] += jnp.dot(p.astype(BF), vt_, preferred_element_type=F32)
        return carry

    lax.fori_loop(0, nt, pv_body, 0)
    for g in range(KV_HEADS):
        og = acc_scr[g] / jnp.sum(l_scr[g], axis=-1, keepdims=True)
        for r in range(G):
            o_ref[0, :, (G * g + r) * HEAD_DIM:(G * g + r + 1) * HEAD_DIM] = og[r * TQ:(r + 1) * TQ].astype(BF)


def _acore_call(q, qi, wi, kib, kb, vb, TQ, KT, pos0, l_valid):
    NB, TT, _ = q.shape
    LP = kb.shape[1]
    NT = LP // KT
    n_top = min(TOPK, l_valid // 4)
    res = lambda w: pl.BlockSpec((1, LP, w), lambda b, j: (b, 0, 0), pipeline_mode=pl.Buffered(1))
    blk = lambda w: pl.BlockSpec((1, TQ, w), lambda b, j: (b, j, 0))
    G = HEAD_GROUP
    return pl.pallas_call(
        functools.partial(_acore_kernel, TQ=TQ, KT=KT, NT=NT, pos0=pos0, l_valid=l_valid, n_top=n_top),
        grid=(NB, TT // TQ),
        in_specs=[blk(Q_W), blk(IDXQ_W), blk(IDX_HEADS), res(IDX_DIM), res(KV_W), res(KV_W)],
        out_specs=blk(Q_W),
        out_shape=jax.ShapeDtypeStruct((NB, TT, Q_W), BF),
        scratch_shapes=[pltpu.VMEM((NT, TQ, KT), jnp.int32), pltpu.VMEM((NT, TQ, KT), jnp.int16),
                        pltpu.VMEM((KV_HEADS, G * TQ, LANE), F32),
                        pltpu.VMEM((KV_HEADS, G * TQ, LANE), F32),
                        pltpu.VMEM((KV_HEADS, G * TQ, HEAD_DIM), F32)],
        compiler_params=_cparams(("arbitrary", "arbitrary"), 56),
        name="attn_core",
    )(q, qi, wi, kib, kb, vb)


def _score_keys(qst, wst, kmat, TQ):
    s = lax.dot_general(qst, kmat, (((1,), (1,)), ((), ())), preferred_element_type=F32)
    s = jnp.maximum(s, 0.0) * wst
    sc = jnp.sum(s.reshape(IDX_HEADS, TQ, kmat.shape[0]), axis=0) + 0.0
    bits = lax.bitcast_convert_type(sc, jnp.int32)
    return jnp.where(bits < 0, bits ^ jnp.int32(0x7FFFFFFF), bits)


def _dec_index_kernel(qi_ref, wi_ref, cki_ref, kin_ref, keys_ref, thr_ref, kscr, *, SB, TQ, LC, KT, pos0, n_top):
    rows = SB * TQ
    width = LC + LANE
    qchunk = (pos0 + lax.broadcasted_iota(jnp.int32, (TQ, 1), 0)) // CHUNK
    for s in range(SB):
        qi = qi_ref[s]
        qst = jnp.concatenate([qi[:, h * IDX_DIM:(h + 1) * IDX_DIM] for h in range(IDX_HEADS)], axis=0)
        wi = wi_ref[s]
        wst = jnp.concatenate([wi[:, h:h + 1] for h in range(IDX_HEADS)], axis=0)
        for c in range(LC // KT):
            kit = cki_ref[s, c * KT:(c + 1) * KT, :].astype(BF)
            kpos = c * KT + lax.broadcasted_iota(jnp.int32, (1, KT), 1)
            key = _score_keys(qst, wst, kit, TQ)
            kscr[s * TQ:(s + 1) * TQ, c * KT:(c + 1) * KT] = jnp.where(kpos // CHUNK <= qchunk, key,
                                                                         jnp.int32(INT_MIN))
        kin = jnp.concatenate([kin_ref[s], jnp.zeros((LANE - TQ, IDX_DIM), BF)], axis=0)
        lane = lax.broadcasted_iota(jnp.int32, (1, LANE), 1)
        key = _score_keys(qst, wst, kin, TQ)
        adm = ((LC + lane) // CHUNK <= qchunk) & (lane < TQ)
        kscr[s * TQ:(s + 1) * TQ, LC:width] = jnp.where(adm, key, jnp.int32(INT_MIN))

    def chunk_pos(c):
        return c * LANE + lax.broadcasted_iota(jnp.int32, (1, LANE), 1)

    def count(pred):
        acc = jnp.zeros((rows, LANE), F32)
        for c in range(width // LANE):
            acc = acc + jnp.where(pred(kscr[:, c * LANE:(c + 1) * LANE], chunk_pos(c)), 1.0, 0.0)
        return jnp.sum(acc, axis=-1, keepdims=True)

    def rewrite(fn):
        for c in range(width // LANE):
            kscr[:, c * LANE:(c + 1) * LANE] = fn(kscr[:, c * LANE:(c + 1) * LANE], chunk_pos(c))

    thr = _topk_threshold(count, rewrite, rows, n_top, width.bit_length())
    thr_ref[...] = thr.reshape(SB, TQ, 1)
    keys_ref[...] = kscr[...].reshape(SB, TQ, width)


def _dec_core_kernel(q_ref, keys_ref, thr_ref, ck_ref, cv_ref, kn_ref, vn_ref, o_ref, *, TQ, LC):
    G = HEAD_GROUP
    q = q_ref[0]
    sel = keys_ref[0] >= thr_ref[0]
    sel_c, sel_n = sel[None, :, 0:LC], sel[None, :, LC:LC + LANE]
    kn_all, vn_all = kn_ref[0], vn_ref[0]
    pad_rows = jnp.zeros((LANE - TQ, HEAD_DIM), BF)
    nt_dims = (((1,), (1,)), ((), ()))
    for g in range(KV_HEADS):
        hs = slice(g * HEAD_DIM, (g + 1) * HEAD_DIM)
        qg = jnp.concatenate([q[:, (G * g + r) * HEAD_DIM:(G * g + r + 1) * HEAD_DIM] for r in range(G)], axis=0)
        kc = ck_ref[0, pl.ds(g, LC, stride=KV_HEADS), :].astype(BF)
        vc = cv_ref[0, pl.ds(g, LC, stride=KV_HEADS), :].astype(BF)
        kn = jnp.concatenate([kn_all[:, hs], pad_rows], axis=0)
        vn = jnp.concatenate([vn_all[:, hs], pad_rows], axis=0)
        sc = lax.dot_general(qg, kc, nt_dims, preferred_element_type=F32).reshape(G, TQ, LC)
        sn = lax.dot_general(qg, kn, nt_dims, preferred_element_type=F32).reshape(G, TQ, LANE)
        sc = jnp.where(sel_c, sc, NEG_BIG).reshape(G * TQ, LC)
        sn = jnp.where(sel_n, sn, NEG_BIG).reshape(G * TQ, LANE)
        m = jnp.maximum(jnp.max(sc, axis=-1, keepdims=True), jnp.max(sn, axis=-1, keepdims=True))
        pc = jnp.exp2(sc - m)
        pn = jnp.exp2(sn - m)
        l = jnp.sum(pc, axis=-1, keepdims=True) + jnp.sum(pn, axis=-1, keepdims=True)
        og = (jnp.dot(pc.astype(BF), vc, preferred_element_type=F32)
              + jnp.dot(pn.astype(BF), vn, preferred_element_type=F32)) / l
        for r in range(G):
            o_ref[0, :, (G * g + r) * HEAD_DIM:(G * g + r + 1) * HEAD_DIM] = og[r * TQ:(r + 1) * TQ].astype(BF)


def _dec_attn_call(q, qi, wi, kib, kb, vb, cache_k, cache_v, cache_kidx, pos0):
    NS, TQ, _ = q.shape
    LC = cache_kidx.shape[1]
    KT, SB = 512, 8
    assert NS % SB == 0 and LC % KT == 0 and TQ <= LANE
    n_top = min(TOPK, (LC + TQ) // 4)
    width = LC + LANE
    blk = lambda n, w: pl.BlockSpec((n, TQ, w), lambda b: (b, 0, 0))
    keys, thr = pl.pallas_call(
        functools.partial(_dec_index_kernel, SB=SB, TQ=TQ, LC=LC, KT=KT, pos0=pos0, n_top=n_top),
        grid=(NS // SB,),
        in_specs=[blk(SB, IDXQ_W), blk(SB, IDX_HEADS),
                  pl.BlockSpec((SB, LC, IDX_DIM), lambda b: (b, 0, 0)), blk(SB, IDX_DIM)],
        out_specs=[blk(SB, width), blk(SB, 1)],
        out_shape=[jax.ShapeDtypeStruct((NS, TQ, width), jnp.int32),
                   jax.ShapeDtypeStruct((NS, TQ, 1), jnp.int32)],
        scratch_shapes=[pltpu.VMEM((SB * TQ, width), jnp.int32)],
        compiler_params=_cparams(("arbitrary",), 48),
        name="attn_dec_index",
    )(qi, wi, cache_kidx, kib)
    cache_spec = pl.BlockSpec((1, LC * KV_HEADS, HEAD_DIM), lambda b: (b, 0, 0))
    return pl.pallas_call(
        functools.partial(_dec_core_kernel, TQ=TQ, LC=LC),
        grid=(NS,),
        in_specs=[blk(1, Q_W), blk(1, width), blk(1, 1), cache_spec, cache_spec, blk(1, KV_W), blk(1, KV_W)],
        out_specs=blk(1, Q_W),
        out_shape=jax.ShapeDtypeStruct((NS, TQ, Q_W), BF),
        compiler_params=_cparams(("arbitrary",), 48),
        name="attn_dec_core",
    )(q, keys, thr, cache_k, cache_v, kb, vb)


def _trunk(x, mods, S, R, TF, TN, W, pos0, past):
    NB, TT, _ = x.shape
    fresh = past is None
    states = {}
    ffn16 = W["ffn16"]

    def ffn(x, mod, g, layer, which, gfin=None):
        nxt = (layer, which + 1) if which == 0 else (layer + 1, 0)
        make_next = fresh and nxt[0] < DEPTH
        y, w16_next = _ffn_call(x, mod, g, ffn16[layer, which], 2 * which, S, R, TF, gfin,
                                (W["ffn32"], *nxt) if make_next else None)
        if make_next:
            ffn16[nxt] = w16_next
        return y

    for i in range(DEPTH):
        kind = i % 4
        mod = mods[i]
        gn = W["g_norm"][i]
        x = ffn(x, mod, gn[0:1], i, 0)
        if kind == 0:
            prefix = None if fresh else jnp.pad(past["pool"], ((0, 0), (1, 0), (0, 0)))
            x, st = _pool_call(x, mod, gn[1:2], W["w_pool"], W["s_pool"], R if fresh else TT, prefix)
            states["pool"] = st[:, 1:]
        elif kind == 1:
            prefix = None if fresh else jnp.pad(past["sconv"], ((0, 0), (SCONV_PAD - 2, 0), (0, 0)))
            z, st = _conv_call(x, mod, gn[1:2], W["w_sc_in"], "sconv", S, R, TN, W["w_sc_conv"],
                               prefix=prefix)
            states["sconv"] = st
            x = _linres_call(z, x, mod, 5, W["w_sc_out"], S, R)
        elif kind == 2:
            reps = 1 if fresh else NB
            tabs = _rope_tables(pos0 + jnp.arange(TT), reps)
            q, k32, kb, v32, vb, qi, ki32, kib, wi = _aproj_call(
                x, mod, gn[1:2], W["w_attn_main"], W["w_attn_tail"], tabs, S, R)
            states["k"], states["v"], states["kidx"] = k32, v32, ki32
            if fresh:
                o = _acore_call(q, qi, wi, kib, kb, vb, 256, 512, pos0, TT)
            else:
                o = _dec_attn_call(q, qi, wi, kib, kb, vb, past["k"], past["v"], past["kidx"], pos0)
            x = _linres_call(o, x, mod, 5, W["w_attn_out"], S, R)
        else:
            prefix = None if fresh else jnp.pad(past["cconv"], ((0, 0), (CCONV_PAD - 30, 0), (0, 0)))
            cv, st = _conv_call(x, mod, gn[1:2], W["w_cm_pw1"], "cconv", S, R, TN, W["w_cm_dw"],
                                pbias=W["b_cm_pw1"], b_dw=W["b_cm_dw"], prefix=prefix)
            states["cconv"] = st
            x = _linres_call(cv, x, mod, 5, W["w_cm_pw2"], S, R,
                             ln=(W["g_cm_ln"], W["b_cm_ln"]), bias=W["b_cm_pw2"])
        gfin = W["g_final"] if i == DEPTH - 1 else None
        x = ffn(x, mod, gn[2:3], i, 1, gfin)
    return x, states


def kernel(x_prompt, x_sample, state_pool, state_sconv, cache_k, cache_v, cache_kidx, state_cconv, c_prompt, c_sample, w_mod, b_mod, g_norm, w_ffn_gate, w_ffn_up, w_ffn_down, w_pool, s_pool, w_sc_in, w_sc_conv, w_sc_out, w_attn_in, w_attn_out, w_cm_pw1, b_cm_pw1, w_cm_dw, b_cm_dw, g_cm_ln, b_cm_ln, w_cm_pw2, b_cm_pw2, g_final):
    B, T, _ = x_prompt.shape
    NS, TS, _ = x_sample.shape
    assert D_FF % 512 == 0 and T % 512 == 0 and B + NS <= MOD_ROWS
    assert w_pool.shape[0] == w_sc_in.shape[0] == w_attn_in.shape[0] == w_cm_pw1.shape[0] == 1

    c_all = jnp.concatenate([c_prompt, c_sample, jnp.zeros((MOD_ROWS - B - NS, D), F32)], axis=0)
    mod_all = _mod_call(c_all, w_mod, b_mod)
    mods_p = [mod_all[i, 0:B].reshape(B, 1, 9 * D) for i in range(DEPTH)]
    mods_s = [mod_all[i, B:B + NS].reshape(NS, 1, 9 * D) for i in range(DEPTH)]

    wa = w_attn_in[0].astype(BF)
    W = dict(
        g_norm=g_norm, g_final=g_final.reshape(1, D),
        ffn32=(w_ffn_gate, w_ffn_up, w_ffn_down),
        ffn16={(0, 0): (w_ffn_gate[0, 0].astype(BF), w_ffn_up[0, 0].astype(BF), w_ffn_down[0, 0].astype(BF))},
        w_pool=w_pool[0].astype(BF), s_pool=s_pool,
        w_sc_in=w_sc_in[0].astype(BF), w_sc_conv=w_sc_conv[0], w_sc_out=w_sc_out[0].astype(BF),
        w_attn_main=wa, w_attn_tail=jnp.pad(wa[:, ATTN_MAIN:], ((0, 0), (0, LANE - ATTN_TAIL))),
        w_attn_out=w_attn_out[0].astype(BF),
        w_cm_pw1=w_cm_pw1[0].astype(BF), b_cm_pw1=b_cm_pw1, w_cm_dw=w_cm_dw[0], b_cm_dw=b_cm_dw,
        g_cm_ln=g_cm_ln, b_cm_ln=b_cm_ln, w_cm_pw2=w_cm_pw2[0].astype(BF), b_cm_pw2=b_cm_pw2,
    )

    y_p, st_p = _trunk(x_prompt, mods_p, 1, 512, 512, 512, W, 0, None)
    past = dict(pool=state_pool[0], sconv=state_sconv[0], cconv=state_cconv[0],
                k=cache_k[0].reshape(NS, -1, HEAD_DIM), v=cache_v[0].reshape(NS, -1, HEAD_DIM),
                kidx=cache_kidx[0])
    y_s, st_s = _trunk(x_sample, mods_s, NS, TS, 512, 512, W, cache_k.shape[2], past)

    def kv(a):
        return a.reshape(1, a.shape[0], a.shape[1], KV_HEADS, HEAD_DIM)

    return (y_p, y_s, st_p["pool"][None], st_s["pool"][None], st_p["sconv"][None], st_s["sconv"][None],
            kv(st_p["k"]), kv(st_s["k"]), kv(st_p["v"]), kv(st_s["v"]),
            st_p["kidx"][None], st_s["kidx"][None], st_p["cconv"][None], st_s["cconv"][None])
```

```python
import functools
import math

import jax
import jax.numpy as jnp
import numpy as np
from jax import lax
from jax.experimental import pallas as pl
from jax.experimental.pallas import tpu as pltpu

D = 2048
D_FF = 5632
DEPTH = 4
NORM_EPS = 1e-6
CHUNK = 64
POOL_WINDOWS = (2, 4, 8, 16)
POOL_GROUP = D // 4
POOL_PAD = 16
SCONV_WIDTH = 3
SCONV_PAD = 8
CCONV_WIDTH = 31
CCONV_PAD = 32
N_HEADS = 16
HEAD_DIM = 128
KV_HEADS = 4
HEAD_GROUP = N_HEADS // KV_HEADS
ROPE_DIM = HEAD_DIM // 4
ROPE_THETA = 500000.0
IDX_HEADS = 16
IDX_DIM = 64
IDX_ROPE_DIM = IDX_DIM // 4
IDX_SCALE = IDX_HEADS ** -0.5 * IDX_DIM ** -0.5
TOPK = 256
Q_W = N_HEADS * HEAD_DIM
KV_W = KV_HEADS * HEAD_DIM
IDXQ_W = IDX_HEADS * IDX_DIM
ATTN_MAIN = Q_W + 2 * KV_W + IDXQ_W
ATTN_TAIL = IDX_DIM + IDX_HEADS
LANE = 128
MOD_ROWS = 40
Q_SCALE = HEAD_DIM ** -0.5 * math.log2(math.e)
INT_MIN = -2 ** 31
NEG_BIG = float("-inf")

BF = jnp.bfloat16
F32 = jnp.float32


BF16_ROWS = 16
SUBLANES = 8
CONV_CHUNK_ROWS = 64
CONV_CHUNK_LANES = 256


def _round_up(n, m):
    return -(-n // m) * m


def _cparams(sem, vmem_mib):
    return pltpu.CompilerParams(dimension_semantics=sem, vmem_limit_bytes=vmem_mib << 20)


def _rms(x, g):
    return x * lax.rsqrt(jnp.mean(x * x, axis=-1, keepdims=True) + NORM_EPS) * g


def _rms_mod(x, g, shift, scale):
    return _rms(x, g) * (1.0 + scale) + shift


def _silu(x):
    return x * jax.nn.sigmoid(x)


def _row_spec(S, R, W):
    return pl.BlockSpec((S, R, W), lambda b, t, n: (b, t, 0))


def _mod_spec(S, col):
    return pl.BlockSpec((S, 1, D), lambda b, t, n: (b, 0, col))


def _const_spec(shape):
    nd = len(shape)
    return pl.BlockSpec(shape, lambda b, t, n: (0,) * nd)


def _mod_kernel(c_ref, w_ref, b_ref, o_ref):
    a = _silu(c_ref[...]).astype(BF)
    o_ref[0] = jnp.dot(a, w_ref[0].astype(BF), preferred_element_type=F32) + b_ref[0]


def _mod_call(c_all, w_mod, b_mod):
    TN = 1024
    N = 9 * D
    return pl.pallas_call(
        _mod_kernel,
        grid=(DEPTH, N // TN),
        in_specs=[pl.BlockSpec((MOD_ROWS, D), lambda i, n: (0, 0)),
                  pl.BlockSpec((1, D, TN), lambda i, n: (i, 0, n)),
                  pl.BlockSpec((1, 1, TN), lambda i, n: (i, 0, n))],
        out_specs=pl.BlockSpec((1, MOD_ROWS, TN), lambda i, n: (i, 0, n)),
        out_shape=jax.ShapeDtypeStruct((DEPTH, MOD_ROWS, N), F32),
        compiler_params=_cparams(("arbitrary", "arbitrary"), 40),
        name="mod",
    )(c_all, w_mod, b_mod.reshape(DEPTH, 1, N))


def _ffn_kernel(*refs, S, R, final, cast_next):
    refs = list(refs)
    x_ref, sh_ref, sc_ref, gt_ref, g_ref, wg_ref, wu_ref, wd_ref = refs[:8]
    rest = refs[8:]
    if final:
        gf_ref = rest[0]
        rest = rest[1:]
    if cast_next:
        next32 = rest[:3]
        o_ref, *next16, h_scr = rest[3:]
    else:
        o_ref, h_scr = rest
    f = pl.program_id(2)
    M = S * R

    @pl.when(f == 0)
    def _():
        h = _rms_mod(x_ref[...], g_ref[...], sh_ref[...], sc_ref[...])
        h_scr[...] = h.reshape(M, D).astype(BF)
        o_ref[...] = jnp.zeros_like(o_ref)

    h = h_scr[...]
    g = jnp.dot(h, wg_ref[...], preferred_element_type=F32)
    u = jnp.dot(h, wu_ref[...], preferred_element_type=F32)
    a = (_silu(g) * u).astype(BF)
    o_ref[...] += jnp.dot(a, wd_ref[...], preferred_element_type=F32).reshape(S, R, D)

    if cast_next:
        for src, dst in zip(next32, next16):
            dst[...] = src[...].astype(BF)

    @pl.when(f == pl.num_programs(2) - 1)
    def _():
        y = x_ref[...] + 0.5 * gt_ref[...] * o_ref[...]
        if final:
            y = _rms(y, gf_ref[...])
        o_ref[...] = y


def _ffn_call(x, mod, g, w16, sub, S, R, TF, gfin=None, next32=None):
    NB, TT, _ = x.shape
    final = gfin is not None
    nb, nt, nf = NB // S, TT // R, D_FF // TF
    in_specs = [_row_spec(S, R, D), _mod_spec(S, 3 * sub), _mod_spec(S, 3 * sub + 1),
                _mod_spec(S, 3 * sub + 2), _const_spec((1, D)),
                pl.BlockSpec((D, TF), lambda b, t, f: (0, f)),
                pl.BlockSpec((D, TF), lambda b, t, f: (0, f)),
                pl.BlockSpec((TF, D), lambda b, t, f: (f, 0))]
    args = [x, mod, mod, mod, g, *w16]
    if final:
        in_specs.append(_const_spec((1, D)))
        args.append(gfin)
    out_specs = [_row_spec(S, R, D)]
    out_shape = [jax.ShapeDtypeStruct(x.shape, F32)]
    if next32 is not None:
        (wg32, wu32, wd32), layer, which = next32
        up_rows, down_rows = D // (nb * nt), D_FF // (nb * nt * nf)
        assert up_rows * nb * nt == D and down_rows * nb * nt * nf == D_FF
        assert up_rows % BF16_ROWS == 0 and down_rows % BF16_ROWS == 0
        up_idx = lambda b, t, f: (b * nt + t, f)
        down_idx = lambda b, t, f: ((b * nt + t) * nf + f, 0)
        in_specs += [pl.BlockSpec((None, None, up_rows, TF), lambda b, t, f: (layer, which, *up_idx(b, t, f))),
                     pl.BlockSpec((None, None, up_rows, TF), lambda b, t, f: (layer, which, *up_idx(b, t, f))),
                     pl.BlockSpec((None, None, down_rows, D), lambda b, t, f: (layer, which, *down_idx(b, t, f)))]
        args += [wg32, wu32, wd32]
        out_specs += [pl.BlockSpec((up_rows, TF), up_idx), pl.BlockSpec((up_rows, TF), up_idx),
                      pl.BlockSpec((down_rows, D), down_idx)]
        out_shape += [jax.ShapeDtypeStruct((D, D_FF), BF), jax.ShapeDtypeStruct((D, D_FF), BF),
                      jax.ShapeDtypeStruct((D_FF, D), BF)]
    y, *w16_next = pl.pallas_call(
        functools.partial(_ffn_kernel, S=S, R=R, final=final, cast_next=next32 is not None),
        grid=(nb, nt, nf),
        in_specs=in_specs,
        out_specs=out_specs,
        out_shape=out_shape,
        scratch_shapes=[pltpu.VMEM((S * R, D), BF)],
        compiler_params=_cparams(("arbitrary", "arbitrary", "arbitrary"), 48),
        name="ffn",
    )(*args)
    return y, tuple(w16_next)


def _linres_kernel(*refs, S, R, ln, bias):
    refs = list(refs)
    a_ref, x_ref, gt_ref, w_ref = refs[:4]
    rest = refs[4:]
    if ln:
        gl_ref, bl_ref = rest[:2]
        rest = rest[2:]
    if bias:
        b_ref = rest[0]
        rest = rest[1:]
    (o_ref,) = rest
    a = a_ref[...]
    if ln:
        a = a.astype(F32)
        mu = jnp.mean(a, axis=-1, keepdims=True)
        ac = a - mu
        a = ac * lax.rsqrt(jnp.mean(ac * ac, axis=-1, keepdims=True) + NORM_EPS)
        a = _silu(a * gl_ref[...] + bl_ref[...])
    a = a.reshape(S * R, a.shape[-1]).astype(BF)
    y = jnp.dot(a, w_ref[...], preferred_element_type=F32)
    if bias:
        y = y + b_ref[...]
    o_ref[...] = x_ref[...] + gt_ref[...] * y.reshape(o_ref.shape)


def _linres_call(a, x, mod, gate_col, w, S, R, ln=None, bias=None):
    NB, TT, K = a.shape
    in_specs = [_row_spec(S, R, K), _row_spec(S, R, D), _mod_spec(S, gate_col),
                pl.BlockSpec((K, D), lambda b, t, n: (0, 0), pipeline_mode=pl.Buffered(1))]
    args = [a, x, mod, w]
    if ln is not None:
        in_specs += [_const_spec((1, K)), _const_spec((1, K))]
        args += list(ln)
    if bias is not None:
        in_specs.append(_const_spec((1, D)))
        args.append(bias)
    return pl.pallas_call(
        functools.partial(_linres_kernel, S=S, R=R, ln=ln is not None, bias=bias is not None),
        grid=(NB // S, TT // R, 1),
        in_specs=in_specs,
        out_specs=_row_spec(S, R, D),
        out_shape=jax.ShapeDtypeStruct(x.shape, F32),
        compiler_params=_cparams(("arbitrary", "arbitrary", "arbitrary"), 56),
        name="linres",
    )(*args)


def _pool_kernel(*refs, R, has_prefix, n_valid_prefix):
    if has_prefix:
        (x_ref, sh_ref, sc_ref, gt_ref, g_ref, wp_ref, sp_ref, pre_ref,
         o_ref, st_ref, hext) = refs
    else:
        (x_ref, sh_ref, sc_ref, gt_ref, g_ref, wp_ref, sp_ref,
         o_ref, st_ref, hext, carry) = refs
    t = pl.program_id(1)
    x = x_ref[0]
    h = _rms_mod(x, g_ref[...], sh_ref[0], sc_ref[0])
    if has_prefix:
        hext[0:POOL_PAD, :] = pre_ref[0]
    else:
        @pl.when(t == 0)
        def _():
            hext[0:POOL_PAD, :] = jnp.zeros((POOL_PAD, D), F32)

        @pl.when(t > 0)
        def _():
            hext[0:POOL_PAD, :] = carry[...]

        carry[...] = h[R - POOL_PAD:R, :]
    hext[POOL_PAD:POOL_PAD + R, :] = h
    st_ref[0] = h[R - POOL_PAD:R, :]
    tpos = t * R + lax.broadcasted_iota(jnp.int32, (R, 1), 0)
    gate = gt_ref[0]
    for gi, w in enumerate(POOL_WINDOWS):
        lo = gi * POOL_GROUP
        hi = lo + POOL_GROUP
        rc = min(R, CONV_CHUNK_ROWS)
        wins = []
        for r0 in range(0, R, rc):
            acc = hext[POOL_PAD + r0:POOL_PAD + r0 + rc, lo:hi]
            for j in range(1, w):
                acc = acc + hext[POOL_PAD + r0 - j:POOL_PAD + r0 - j + rc, lo:hi]
            wins.append(acc)
        win = jnp.concatenate(wins, axis=0) if len(wins) > 1 else wins[0]
        cnt = jnp.minimum(tpos + 1 + n_valid_prefix, w).astype(F32)
        pooled = (win / cnt - h[:, lo:hi]).astype(BF)
        y = jnp.dot(pooled, wp_ref[gi], preferred_element_type=F32) * sp_ref[:, lo:hi]
        o_ref[0, :, lo:hi] = x[:, lo:hi] + gate[:, lo:hi] * y


def _pool_call(x, mod, g, w_pool, s_pool, R, prefix=None):
    NB, TT, _ = x.shape
    has_prefix = prefix is not None
    in_specs = [_row_spec(1, R, D), _mod_spec(1, 3), _mod_spec(1, 4), _mod_spec(1, 5),
                _const_spec((1, D)), _const_spec((4, POOL_GROUP, POOL_GROUP)), _const_spec((1, D))]
    args = [x, mod, mod, mod, g, w_pool, s_pool]
    scratch = [pltpu.VMEM((R + POOL_PAD, D), F32)]
    if has_prefix:
        in_specs.append(pl.BlockSpec((1, POOL_PAD, D), lambda b, t, n: (b, 0, 0)))
        args.append(prefix)
    else:
        scratch.append(pltpu.VMEM((POOL_PAD, D), F32))
    return pl.pallas_call(
        functools.partial(_pool_kernel, R=R, has_prefix=has_prefix,
                          n_valid_prefix=POOL_PAD - 1 if has_prefix else 0),
        grid=(NB, TT // R, 1),
        in_specs=in_specs,
        out_specs=[_row_spec(1, R, D), pl.BlockSpec((1, POOL_PAD, D), lambda b, t, n: (b, 0, 0))],
        out_shape=[jax.ShapeDtypeStruct(x.shape, F32), jax.ShapeDtypeStruct((NB, POOL_PAD, D), F32)],
        scratch_shapes=scratch,
        compiler_params=_cparams(("arbitrary", "arbitrary", "arbitrary"), 40),
        name="pool",
    )(*args)


def _conv_kernel(*refs, S, R, TN, mode, has_prefix):
    refs = list(refs)
    x_ref, sh_ref, sc_ref, g_ref = refs[:4]
    rest = refs[4:]
    nproj = 3 if mode == "sconv" else 2
    w_refs = rest[:nproj]
    rest = rest[nproj:]
    if mode == "cconv":
        pb_refs = rest[:2]
        wc_ref, bdw_ref = rest[2:4]
        rest = rest[4:]
        width, pad = CCONV_WIDTH, CCONV_PAD
    else:
        wc_ref = rest[0]
        rest = rest[1:]
        width, pad = SCONV_WIDTH, SCONV_PAD
    if has_prefix:
        pre_ref = rest[0]
        z_ref, st_ref, h_scr, uext, vbuf = rest[1:]
    else:
        z_ref, st_ref, h_scr, uext, vbuf, carry = rest
    t = pl.program_id(1)
    n = pl.program_id(2)
    M = S * R

    @pl.when(n == 0)
    def _():
        h = _rms_mod(x_ref[...], g_ref[...], sh_ref[...], sc_ref[...])
        h_scr[...] = h.reshape(M, D).astype(BF)

    h = h_scr[...]
    proj = [jnp.dot(h, w_ref[...], preferred_element_type=F32) for w_ref in w_refs]
    if mode == "sconv":
        bq, cq, vq = proj
        u = cq * vq
    else:
        a = proj[0] + pb_refs[0][...]
        gq = proj[1] + pb_refs[1][...]
        u = a * jax.nn.sigmoid(gq)
    u3 = u.reshape(S, R, TN)
    if has_prefix:
        uext[:, 0:pad, :] = pre_ref[...]
    else:
        @pl.when(t == 0)
        def _():
            uext[:, 0:pad, :] = jnp.zeros((S, pad, TN), F32)

        @pl.when(t > 0)
        def _():
            uext[:, 0:pad, :] = carry[n]
    uext[:, pad:pad + R, :] = u3
    tail = uext[:, R:R + pad, :]
    if not has_prefix:
        carry[n] = tail
    st_ref[...] = tail
    wc = wc_ref[...]
    if mode == "sconv":
        bq3 = bq.reshape(S, R, TN)
    base = pad - (width - 1)
    rc = min(R, CONV_CHUNK_ROWS)
    sc_ = max(1, CONV_CHUNK_ROWS // rc)
    for s0 in range(0, S, sc_):
        for r0 in range(0, R, rc):
            for l0 in range(0, TN, CONV_CHUNK_LANES):
                ls = slice(l0, l0 + CONV_CHUNK_LANES)
                acc = None
                for sub in range(min(SUBLANES, width)):
                    taps = range(sub, width, SUBLANES)
                    start = r0 + base + sub
                    span = rc + taps[-1] - sub
                    vbuf[:, 0:span, :] = uext[s0:s0 + sc_, start:start + span, ls]
                    for j in taps:
                        term = wc[j:j + 1, ls] * vbuf[:, j - sub:j - sub + rc, :]
                        acc = term if acc is None else acc + term
                if mode == "sconv":
                    z_ref[s0:s0 + sc_, r0:r0 + rc, ls] = (bq3[s0:s0 + sc_, r0:r0 + rc, ls] * acc).astype(z_ref.dtype)
                else:
                    z_ref[s0:s0 + sc_, r0:r0 + rc, ls] = acc + bdw_ref[:, ls]


def _conv_call(x, mod, g, w, mode, S, R, TN, w_conv, pbias=None, b_dw=None, prefix=None):
    NB, TT, _ = x.shape
    has_prefix = prefix is not None
    nproj = 3 if mode == "sconv" else 2
    width, pad = (SCONV_WIDTH, SCONV_PAD) if mode == "sconv" else (CCONV_WIDTH, CCONV_PAD)
    nN = D // TN
    in_specs = [_row_spec(S, R, D), _mod_spec(S, 3), _mod_spec(S, 4), _const_spec((1, D))]
    args = [x, mod, mod, g]
    for k in range(nproj):
        in_specs.append(pl.BlockSpec((D, TN), lambda b, t, n, k=k: (0, k * nN + n)))
        args.append(w)
    if mode == "cconv":
        for k in range(2):
            in_specs.append(pl.BlockSpec((1, TN), lambda b, t, n, k=k: (0, k * nN + n)))
            args.append(pbias)
    in_specs.append(pl.BlockSpec((width, TN), lambda b, t, n: (0, n)))
    args.append(w_conv)
    if mode == "cconv":
        in_specs.append(pl.BlockSpec((1, TN), lambda b, t, n: (0, n)))
        args.append(b_dw)
    rc = min(R, CONV_CHUNK_ROWS)
    scratch = [pltpu.VMEM((S * R, D), BF), pltpu.VMEM((S, R + pad, TN), F32),
               pltpu.VMEM((max(1, CONV_CHUNK_ROWS // rc), rc + _round_up(width, SUBLANES), CONV_CHUNK_LANES), F32)]
    if has_prefix:
        in_specs.append(pl.BlockSpec((S, pad, TN), lambda b, t, n: (b, 0, n)))
        args.append(prefix)
    else:
        scratch.append(pltpu.VMEM((nN, S, pad, TN), F32))
    zdt = BF if mode == "sconv" else F32
    z, tails = pl.pallas_call(
        functools.partial(_conv_kernel, S=S, R=R, TN=TN, mode=mode, has_prefix=has_prefix),
        grid=(NB // S, TT // R, nN),
        in_specs=in_specs,
        out_specs=[pl.BlockSpec((S, R, TN), lambda b, t, n: (b, t, n)),
                   pl.BlockSpec((S, pad, TN), lambda b, t, n: (b, t, n))],
        out_shape=[jax.ShapeDtypeStruct(x.shape, zdt),
                   jax.ShapeDtypeStruct((NB, (TT // R) * pad, D), F32)],
        scratch_shapes=scratch,
        compiler_params=_cparams(("arbitrary", "arbitrary", "arbitrary"), 40),
        name=mode,
    )(*args)
    return z, tails[:, -(width - 1):]


def _rope_lanes(x, cos, sin, period, half):
    lane = lax.broadcasted_iota(jnp.int32, x.shape, x.ndim - 1) % period
    width = x.shape[-1]
    partner = jnp.where(lane < half, pltpu.roll(x, width - half, x.ndim - 1),
                        pltpu.roll(x, half, x.ndim - 1))
    return x * cos + partner * sin


def _aproj_kernel(x_ref, sh_ref, sc_ref, g_ref, w_ref, wt_ref,
                  ch_ref, sh_h_ref, ci_ref, si_ref, ct_ref, st_ref,
                  q_ref, k32_ref, kb_ref, v32_ref, vb_ref, qi_ref, ki32_ref, kib_ref, wi_ref,
                  h_scr, *, S, R):
    n = pl.program_id(2)
    M = S * R
    TN = 4 * HEAD_DIM

    @pl.when(n == 0)
    def _():
        h = _rms_mod(x_ref[...], g_ref[...], sh_ref[...], sc_ref[...])
        h_scr[...] = h.reshape(M, D).astype(BF)

    def heads_rope(p):
        cos, sin = ch_ref[...], sh_h_ref[...]
        return jnp.concatenate(
            [_rope_lanes(p[:, i * HEAD_DIM:(i + 1) * HEAD_DIM], cos, sin, HEAD_DIM, ROPE_DIM // 2)
             for i in range(TN // HEAD_DIM)], axis=-1)

    @pl.when(n < 4)
    def _():
        p = jnp.dot(h_scr[...], w_ref[...], preferred_element_type=F32)
        q = heads_rope(p) * Q_SCALE
        q_ref[...] = q.reshape(S, R, TN).astype(BF)

    @pl.when(n == 4)
    def _():
        p = jnp.dot(h_scr[...], w_ref[...], preferred_element_type=F32)
        k = heads_rope(p).reshape(S, R, TN)
        k32_ref[...] = k
        kb_ref[...] = k.astype(BF)

    @pl.when(n == 5)
    def _():
        v = jnp.dot(h_scr[...], w_ref[...], preferred_element_type=F32).reshape(S, R, TN)
        v32_ref[...] = v
        vb_ref[...] = v.astype(BF)

    @pl.when((n == 6) | (n == 7))
    def _():
        p = jnp.dot(h_scr[...], w_ref[...], preferred_element_type=F32)
        cos, sin = ci_ref[...], si_ref[...]
        qi = jnp.concatenate(
            [_rope_lanes(p[:, i * LANE:(i + 1) * LANE], cos, sin, IDX_DIM, IDX_ROPE_DIM // 2)
             for i in range(TN // LANE)], axis=-1)
        qi_ref[...] = qi.reshape(S, R, TN).astype(BF)

    @pl.when(n == 8)
    def _():
        p = jnp.dot(h_scr[...], wt_ref[...], preferred_element_type=F32)
        tl = _rope_lanes(p, ct_ref[...], st_ref[...], LANE, IDX_ROPE_DIM // 2)
        ki = tl[:, 0:IDX_DIM].reshape(S, R, IDX_DIM)
        ki32_ref[...] = ki
        kib_ref[...] = ki.astype(BF)
        wi_ref[...] = (tl[:, IDX_DIM:IDX_DIM + IDX_HEADS] * IDX_SCALE).reshape(S, R, IDX_HEADS)


def _aproj_call(x, mod, g, w_main, w_tail, tabs, S, R):
    NB, TT, _ = x.shape
    M = S * R
    TN = 4 * HEAD_DIM
    tab_spec = pl.BlockSpec((M, LANE), lambda b, t, n: (t, 0))

    def out(width, dt, idx=lambda b, t, n: (b, t, 0)):
        return (pl.BlockSpec((S, R, min(width, TN)), idx), jax.ShapeDtypeStruct((NB, TT, width), dt))

    outs = [out(Q_W, BF, lambda b, t, n: (b, t, jnp.minimum(n, 3))),
            out(KV_W, F32), out(KV_W, BF), out(KV_W, F32), out(KV_W, BF),
            out(IDXQ_W, BF, lambda b, t, n: (b, t, jnp.clip(n - 6, 0, 1))),
            out(IDX_DIM, F32), out(IDX_DIM, BF), out(IDX_HEADS, F32)]
    return pl.pallas_call(
        functools.partial(_aproj_kernel, S=S, R=R),
        grid=(NB // S, TT // R, 9),
        in_specs=[_row_spec(S, R, D), _mod_spec(S, 3), _mod_spec(S, 4), _const_spec((1, D)),
                  pl.BlockSpec((D, TN), lambda b, t, n: (0, jnp.minimum(n, ATTN_MAIN // TN - 1))),
                  _const_spec((D, LANE))] + [tab_spec] * 6,
        out_specs=[o[0] for o in outs],
        out_shape=[o[1] for o in outs],
        scratch_shapes=[pltpu.VMEM((M, D), BF)],
        compiler_params=_cparams(("arbitrary", "arbitrary", "arbitrary"), 40),
        name="attn_proj",
    )(x, mod, mod, g, w_main, w_tail, *tabs)


def _rope_tables(pos, reps):
    def tab(rot_dim, period, lanes_used):
        half = rot_dim // 2
        inv = jnp.exp(-math.log(ROPE_THETA) * jnp.arange(half, dtype=F32) * (2.0 / rot_dim))
        ang = pos.astype(F32)[:, None] * inv[None, :]
        cos, sin = jnp.cos(ang), jnp.sin(ang)
        n = pos.shape[0]
        one = jnp.ones((n, period - rot_dim), F32)
        zero = jnp.zeros((n, period - rot_dim), F32)
        c = jnp.concatenate([cos, cos, one], axis=1)
        s = jnp.concatenate([-sin, sin, zero], axis=1)
        c = jnp.tile(c, (1, lanes_used // period))
        s = jnp.tile(s, (1, lanes_used // period))
        if lanes_used < LANE:
            c = jnp.concatenate([c, jnp.ones((n, LANE - lanes_used), F32)], axis=1)
            s = jnp.concatenate([s, jnp.zeros((n, LANE - lanes_used), F32)], axis=1)
        return [jnp.tile(c, (reps, 1)), jnp.tile(s, (reps, 1))]

    return tab(ROPE_DIM, HEAD_DIM, LANE) + tab(IDX_ROPE_DIM, IDX_DIM, LANE) + tab(IDX_ROPE_DIM, IDX_DIM, IDX_DIM)


def _topk_threshold(count, rewrite, rows, n_top, pos_bits, halves=None):
    n_top = float(n_top)

    def bit_step(i, carry, nbits, count_ge):
        thr, n_ge = carry
        cand = jnp.where(i == 0, jnp.zeros_like(thr), thr | jnp.left_shift(jnp.int32(1), nbits - 1 - i))
        n_cand = count_ge(cand)
        take = n_cand >= n_top
        return jnp.where(take, cand, thr), jnp.where(take, n_cand, n_ge)

    n_ge = jnp.zeros((rows, 1), F32)
    if halves is None:
        count32 = lambda cand: count(lambda k, p: k >= cand)
        thr, n_ge = lax.fori_loop(0, 32, functools.partial(bit_step, nbits=32, count_ge=count32),
                                  (jnp.full((rows, 1), INT_MIN, jnp.int32), n_ge))
    else:
        count16, keep_low_halves = halves
        lowest = jnp.full((rows, 1), -2 ** 15, jnp.int32)
        hi, n_ge = lax.fori_loop(0, 16, functools.partial(bit_step, nbits=16, count_ge=count16), (lowest, n_ge))
        n_above = count16(hi, strict=True)
        keep_low_halves(hi)
        lo, n_ge = lax.fori_loop(0, 16, functools.partial(bit_step, nbits=16,
                                                          count_ge=lambda c: n_above + count16(c)), (lowest, n_ge))
        thr = jnp.left_shift(hi, 16) | (lo + 2 ** 15)
    excess = n_ge > n_top

    @pl.when(jnp.max(jnp.where(excess, 1, 0)) > 0)
    def _():
        quota = n_top - count(lambda k, p: k > thr)

        def pos_body(i, cut):
            cand = cut | jnp.left_shift(jnp.int32(1), pos_bits - 1 - i)
            return jnp.where(count(lambda k, p: (k == thr) & (p < cand)) < quota, cand, cut)

        cut = lax.fori_loop(0, pos_bits, pos_body, jnp.zeros((rows, 1), jnp.int32))
        rewrite(lambda k, p: jnp.where((k == thr) & (p > cut), jnp.int32(INT_MIN), k))

    return jnp.maximum(thr, jnp.int32(INT_MIN + 1))


def _acore_kernel(q_ref, qi_ref, wi_ref, ki_ref, kb_ref, vb_ref, o_ref,
                  keys, keys16, m_scr, l_scr, acc_scr, *, TQ, KT, NT, pos0, l_valid, n_top):
    j = pl.program_id(1)
    q0 = pos0 + j * TQ
    last_key = ((q0 + TQ - 1) // CHUNK) * CHUNK + CHUNK - 1
    nt = jnp.minimum(last_key // KT + 1, NT)
    qchunk = (q0 + lax.broadcasted_iota(jnp.int32, (TQ, 1), 0)) // CHUNK

    qi = qi_ref[0]
    qst = jnp.concatenate([qi[:, h * IDX_DIM:(h + 1) * IDX_DIM] for h in range(IDX_HEADS)], axis=0)
    wi = wi_ref[0]
    wst = jnp.concatenate([wi[:, h:h + 1] for h in range(IDX_HEADS)], axis=0)

    def score_body(kt, carry):
        start = pl.multiple_of(kt * KT, KT)
        key = _score_keys(qst, wst, ki_ref[0, pl.ds(start, KT), :], TQ)
        kpos = start + lax.broadcasted_iota(jnp.int32, (1, KT), 1)
        adm = (kpos // CHUNK <= qchunk) & (kpos < l_valid)
        key = jnp.where(adm, key, jnp.int32(INT_MIN))
        keys[kt] = key
        keys16[kt] = jnp.right_shift(key, 16).astype(jnp.int16)
        return carry

    lax.fori_loop(0, nt, score_body, 0)

    def count16(cand, strict=False):
        cand16 = cand.astype(jnp.int16)

        def body(kt, acc):
            k16 = keys16[kt]
            hit = jnp.where(k16 > cand16 if strict else k16 >= cand16, jnp.int16(1), jnp.int16(0))
            for c in range(KT // LANE):
                acc = acc + hit[:, c * LANE:(c + 1) * LANE]
            return acc
        acc = lax.fori_loop(0, nt, body, jnp.zeros((TQ, LANE), jnp.int16))
        return jnp.sum(acc.astype(F32), axis=-1, keepdims=True)

    def keep_low_halves(hi):
        hi16 = hi.astype(jnp.int16)

        def body(kt, carry):
            low = jnp.right_shift(jnp.left_shift(keys[kt], 16) ^ jnp.int32(INT_MIN), 16).astype(jnp.int16)
            keys16[kt] = jnp.where(keys16[kt] == hi16, low, jnp.int16(-2 ** 15))
            return carry
        lax.fori_loop(0, nt, body, 0)

    def tile_pos(kt):
        return kt * KT + lax.broadcasted_iota(jnp.int32, (1, KT), 1)

    def count(pred):
        def body(kt, acc):
            hit = jnp.where(pred(keys[kt], tile_pos(kt)), 1.0, 0.0)
            for c in range(KT // LANE):
                acc = acc + hit[:, c * LANE:(c + 1) * LANE]
            return acc
        acc = lax.fori_loop(0, nt, body, jnp.zeros((TQ, LANE), F32))
        return jnp.sum(acc, axis=-1, keepdims=True)

    def rewrite(fn):
        def body(kt, carry):
            keys[kt] = fn(keys[kt], tile_pos(kt))
            return carry
        lax.fori_loop(0, nt, body, 0)

    thr = _topk_threshold(count, rewrite, TQ, n_top, (NT * KT).bit_length(), (count16, keep_low_halves))

    q = q_ref[0]
    G = HEAD_GROUP
    qgs = [jnp.concatenate([q[:, (G * g + r) * HEAD_DIM:(G * g + r + 1) * HEAD_DIM] for r in range(G)],
                           axis=0) for g in range(KV_HEADS)]

    def scores(kt, g):
        start = pl.multiple_of(kt * KT, KT)
        kt_ = kb_ref[0, pl.ds(start, KT), g * HEAD_DIM:(g + 1) * HEAD_DIM]
        s = lax.dot_general(qgs[g], kt_, (((1,), (1,)), ((), ())), preferred_element_type=F32)
        return s.reshape(G, TQ, KT)

    def lane_fold(x, op):
        part = x[:, 0:LANE]
        for c in range(1, KT // LANE):
            part = op(part, x[:, c * LANE:(c + 1) * LANE])
        return part

    m_scr[...] = jnp.full((KV_HEADS, G * TQ, LANE), NEG_BIG, F32)

    def max_body(kt, carry):
        sel = (keys[kt] >= thr)[None]
        for g in range(KV_HEADS):
            s = jnp.where(sel, scores(kt, g), NEG_BIG).reshape(G * TQ, KT)
            m_scr[g] = jnp.maximum(m_scr[g], lane_fold(s, jnp.maximum))
        return carry

    lax.fori_loop(0, nt, max_body, 0)
    ms = [jnp.max(m_scr[g], axis=-1, keepdims=True).reshape(G, TQ, 1) for g in range(KV_HEADS)]
    l_scr[...] = jnp.zeros((KV_HEADS, G * TQ, LANE), F32)
    acc_scr[...] = jnp.zeros((KV_HEADS, G * TQ, HEAD_DIM), F32)

    def pv_body(kt, carry):
        start = pl.multiple_of(kt * KT, KT)
        sel = (keys[kt] >= thr)[None]
        for g in range(KV_HEADS):
            vt_ = vb_ref[0, pl.ds(start, KT), g * HEAD_DIM:(g + 1) * HEAD_DIM]
            p = jnp.where(sel, jnp.exp2(scores(kt, g) - ms[g]), 0.0).reshape(G * TQ, KT)
            l_scr[g] += lane_fold(p, jnp.add)
            acc_scr[g] += jnp.dot(p.astype(BF), vt_, preferred_element_type=F32)
        return carry

    lax.fori_loop(0, nt, pv_body, 0)
    for g in range(KV_HEADS):
        og = acc_scr[g] / jnp.sum(l_scr[g], axis=-1, keepdims=True)
        for r in range(G):
            o_ref[0, :, (G * g + r) * HEAD_DIM:(G * g + r + 1) * HEAD_DIM] = og[r * TQ:(r + 1) * TQ].astype(BF)


def _acore_call(q, qi, wi, kib, kb, vb, TQ, KT, pos0, l_valid):
    NB, TT, _ = q.shape
    LP = kb.shape[1]
    NT = LP // KT
    n_top = min(TOPK, l_valid // 4)
    res = lambda w: pl.BlockSpec((1, LP, w), lambda b, j: (b, 0, 0), pipeline_mode=pl.Buffered(1))
    blk = lambda w: pl.BlockSpec((1, TQ, w), lambda b, j: (b, j, 0))
    G = HEAD_GROUP
    return pl.pallas_call(
        functools.partial(_acore_kernel, TQ=TQ, KT=KT, NT=NT, pos0=pos0, l_valid=l_valid, n_top=n_top),
        grid=(NB, TT // TQ),
        in_specs=[blk(Q_W), blk(IDXQ_W), blk(IDX_HEADS), res(IDX_DIM), res(KV_W), res(KV_W)],
        out_specs=blk(Q_W),
        out_shape=jax.ShapeDtypeStruct((NB, TT, Q_W), BF),
        scratch_shapes=[pltpu.VMEM((NT, TQ, KT), jnp.int32), pltpu.VMEM((NT, TQ, KT), jnp.int16),
                        pltpu.VMEM((KV_HEADS, G * TQ, LANE), F32),
                        pltpu.VMEM((KV_HEADS, G * TQ, LANE), F32),
                        pltpu.VMEM((KV_HEADS, G * TQ, HEAD_DIM), F32)],
        compiler_params=_cparams(("arbitrary", "arbitrary"), 56),
        name="attn_core",
    )(q, qi, wi, kib, kb, vb)


def _score_keys(qst, wst, kmat, TQ):
    s = lax.dot_general(qst, kmat, (((1,), (1,)), ((), ())), preferred_element_type=F32)
    s = jnp.maximum(s, 0.0) * wst
    sc = jnp.sum(s.reshape(IDX_HEADS, TQ, kmat.shape[0]), axis=0) + 0.0
    bits = lax.bitcast_convert_type(sc, jnp.int32)
    return jnp.where(bits < 0, bits ^ jnp.int32(0x7FFFFFFF), bits)


def _dec_index_kernel(qi_ref, wi_ref, cki_ref, kin_ref, keys_ref, thr_ref, kscr, *, SB, TQ, LC, KT, pos0, n_top):
    rows = SB * TQ
    width = LC + LANE
    qchunk = (pos0 + lax.broadcasted_iota(jnp.int32, (TQ, 1), 0)) // CHUNK
    for s in range(SB):
        qi = qi_ref[s]
        qst = jnp.concatenate([qi[:, h * IDX_DIM:(h + 1) * IDX_DIM] for h in range(IDX_HEADS)], axis=0)
        wi = wi_ref[s]
        wst = jnp.concatenate([wi[:, h:h + 1] for h in range(IDX_HEADS)], axis=0)
        for c in range(LC // KT):
            kit = cki_ref[s, c * KT:(c + 1) * KT, :].astype(BF)
            kpos = c * KT + lax.broadcasted_iota(jnp.int32, (1, KT), 1)
            key = _score_keys(qst, wst, kit, TQ)
            kscr[s * TQ:(s + 1) * TQ, c * KT:(c + 1) * KT] = jnp.where(kpos // CHUNK <= qchunk, key,
                                                                         jnp.int32(INT_MIN))
        kin = jnp.concatenate([kin_ref[s], jnp.zeros((LANE - TQ, IDX_DIM), BF)], axis=0)
        lane = lax.broadcasted_iota(jnp.int32, (1, LANE), 1)
        key = _score_keys(qst, wst, kin, TQ)
        adm = ((LC + lane) // CHUNK <= qchunk) & (lane < TQ)
        kscr[s * TQ:(s + 1) * TQ, LC:width] = jnp.where(adm, key, jnp.int32(INT_MIN))

    def chunk_pos(c):
        return c * LANE + lax.broadcasted_iota(jnp.int32, (1, LANE), 1)

    def count(pred):
        acc = jnp.zeros((rows, LANE), F32)
        for c in range(width // LANE):
            acc = acc + jnp.where(pred(kscr[:, c * LANE:(c + 1) * LANE], chunk_pos(c)), 1.0, 0.0)
        return jnp.sum(acc, axis=-1, keepdims=True)

    def rewrite(fn):
        for c in range(width // LANE):
            kscr[:, c * LANE:(c + 1) * LANE] = fn(kscr[:, c * LANE:(c + 1) * LANE], chunk_pos(c))

    thr = _topk_threshold(count, rewrite, rows, n_top, width.bit_length())
    thr_ref[...] = thr.reshape(SB, TQ, 1)
    keys_ref[...] = kscr[...].reshape(SB, TQ, width)


def _dec_core_kernel(q_ref, keys_ref, thr_ref, ck_ref, cv_ref, kn_ref, vn_ref, o_ref, *, TQ, LC):
    G = HEAD_GROUP
    q = q_ref[0]
    sel = keys_ref[0] >= thr_ref[0]
    sel_c, sel_n = sel[None, :, 0:LC], sel[None, :, LC:LC + LANE]
    kn_all, vn_all = kn_ref[0], vn_ref[0]
    pad_rows = jnp.zeros((LANE - TQ, HEAD_DIM), BF)
    nt_dims = (((1,), (1,)), ((), ()))
    for g in range(KV_HEADS):
        hs = slice(g * HEAD_DIM, (g + 1) * HEAD_DIM)
        qg = jnp.concatenate([q[:, (G * g + r) * HEAD_DIM:(G * g + r + 1) * HEAD_DIM] for r in range(G)], axis=0)
        kc = ck_ref[0, pl.ds(g, LC, stride=KV_HEADS), :].astype(BF)
        vc = cv_ref[0, pl.ds(g, LC, stride=KV_HEADS), :].astype(BF)
        kn = jnp.concatenate([kn_all[:, hs], pad_rows], axis=0)
        vn = jnp.concatenate([vn_all[:, hs], pad_rows], axis=0)
        sc = lax.dot_general(qg, kc, nt_dims, preferred_element_type=F32).reshape(G, TQ, LC)
        sn = lax.dot_general(qg, kn, nt_dims, preferred_element_type=F32).reshape(G, TQ, LANE)
        sc = jnp.where(sel_c, sc, NEG_BIG).reshape(G * TQ, LC)
        sn = jnp.where(sel_n, sn, NEG_BIG).reshape(G * TQ, LANE)
        m = jnp.maximum(jnp.max(sc, axis=-1, keepdims=True), jnp.max(sn, axis=-1, keepdims=True))
        pc = jnp.exp2(sc - m)
        pn = jnp.exp2(sn - m)
        l = jnp.sum(pc, axis=-1, keepdims=True) + jnp.sum(pn, axis=-1, keepdims=True)
        og = (jnp.dot(pc.astype(BF), vc, preferred_element_type=F32)
              + jnp.dot(pn.astype(BF), vn, preferred_element_type=F32)) / l
        for r in range(G):
            o_ref[0, :, (G * g + r) * HEAD_DIM:(G * g + r + 1) * HEAD_DIM] = og[r * TQ:(r + 1) * TQ].astype(BF)


def _dec_attn_call(q, qi, wi, kib, kb, vb, cache_k, cache_v, cache_kidx, pos0):
    NS, TQ, _ = q.shape
    LC = cache_kidx.shape[1]
    KT, SB = 512, 8
    assert NS % SB == 0 and LC % KT == 0 and TQ <= LANE
    n_top = min(TOPK, (LC + TQ) // 4)
    width = LC + LANE
    blk = lambda n, w: pl.BlockSpec((n, TQ, w), lambda b: (b, 0, 0))
    keys, thr = pl.pallas_call(
        functools.partial(_dec_index_kernel, SB=SB, TQ=TQ, LC=LC, KT=KT, pos0=pos0, n_top=n_top),
        grid=(NS // SB,),
        in_specs=[blk(SB, IDXQ_W), blk(SB, IDX_HEADS),
                  pl.BlockSpec((SB, LC, IDX_DIM), lambda b: (b, 0, 0)), blk(SB, IDX_DIM)],
        out_specs=[blk(SB, width), blk(SB, 1)],
        out_shape=[jax.ShapeDtypeStruct((NS, TQ, width), jnp.int32),
                   jax.ShapeDtypeStruct((NS, TQ, 1), jnp.int32)],
        scratch_shapes=[pltpu.VMEM((SB * TQ, width), jnp.int32)],
        compiler_params=_cparams(("arbitrary",), 48),
        name="attn_dec_index",
    )(qi, wi, cache_kidx, kib)
    cache_spec = pl.BlockSpec((1, LC * KV_HEADS, HEAD_DIM), lambda b: (b, 0, 0))
    return pl.pallas_call(
        functools.partial(_dec_core_kernel, TQ=TQ, LC=LC),
        grid=(NS,),
        in_specs=[blk(1, Q_W), blk(1, width), blk(1, 1), cache_spec, cache_spec, blk(1, KV_W), blk(1, KV_W)],
        out_specs=blk(1, Q_W),
        out_shape=jax.ShapeDtypeStruct((NS, TQ, Q_W), BF),
        compiler_params=_cparams(("arbitrary",), 48),
        name="attn_dec_core",
    )(q, keys, thr, cache_k, cache_v, kb, vb)


def _trunk(x, mods, S, R, TF, TN, W, pos0, past):
    NB, TT, _ = x.shape
    fresh = past is None
    states = {}
    ffn16 = W["ffn16"]

    def ffn(x, mod, g, layer, which, gfin=None):
        nxt = (layer, which + 1) if which == 0 else (layer + 1, 0)
        make_next = fresh and nxt[0] < DEPTH
        y, w16_next = _ffn_call(x, mod, g, ffn16[layer, which], 2 * which, S, R, TF, gfin,
                                (W["ffn32"], *nxt) if make_next else None)
        if make_next:
            ffn16[nxt] = w16_next
        return y

    for i in range(DEPTH):
        kind = i % 4
        mod = mods[i]
        gn = W["g_norm"][i]
        x = ffn(x, mod, gn[0:1], i, 0)
        if kind == 0:
            prefix = None if fresh else jnp.pad(past["pool"], ((0, 0), (1, 0), (0, 0)))
            x, st = _pool_call(x, mod, gn[1:2], W["w_pool"], W["s_pool"], R if fresh else TT, prefix)
            states["pool"] = st[:, 1:]
        elif kind == 1:
            prefix = None if fresh else jnp.pad(past["sconv"], ((0, 0), (SCONV_PAD - 2, 0), (0, 0)))
            z, st = _conv_call(x, mod, gn[1:2], W["w_sc_in"], "sconv", S, R, TN, W["w_sc_conv"],
                               prefix=prefix)
            states["sconv"] = st
            x = _linres_call(z, x, mod, 5, W["w_sc_out"], S, R)
        elif kind == 2:
            reps = 1 if fresh else NB
            tabs = _rope_tables(pos0 + jnp.arange(TT), reps)
            q, k32, kb, v32, vb, qi, ki32, kib, wi = _aproj_call(
                x, mod, gn[1:2], W["w_attn_main"], W["w_attn_tail"], tabs, S, R)
            states["k"], states["v"], states["kidx"] = k32, v32, ki32
            if fresh:
                o = _acore_call(q, qi, wi, kib, kb, vb, 256, 512, pos0, TT)
            else:
                o = _dec_attn_call(q, qi, wi, kib, kb, vb, past["k"], past["v"], past["kidx"], pos0)
            x = _linres_call(o, x, mod, 5, W["w_attn_out"], S, R)
        else:
            prefix = None if fresh else jnp.pad(past["cconv"], ((0, 0), (CCONV_PAD - 30, 0), (0, 0)))
            cv, st = _conv_call(x, mod, gn[1:2], W["w_cm_pw1"], "cconv", S, R, TN, W["w_cm_dw"],
                                pbias=W["b_cm_pw1"], b_dw=W["b_cm_dw"], prefix=prefix)
            states["cconv"] = st
            x = _linres_call(cv, x, mod, 5, W["w_cm_pw2"], S, R,
                             ln=(W["g_cm_ln"], W["b_cm_ln"]), bias=W["b_cm_pw2"])
        gfin = W["g_final"] if i == DEPTH - 1 else None
        x = ffn(x, mod, gn[2:3], i, 1, gfin)
    return x, states


def kernel(x_prompt, x_sample, state_pool, state_sconv, cache_k, cache_v, cache_kidx, state_cconv, c_prompt, c_sample, w_mod, b_mod, g_norm, w_ffn_gate, w_ffn_up, w_ffn_down, w_pool, s_pool, w_sc_in, w_sc_conv, w_sc_out, w_attn_in, w_attn_out, w_cm_pw1, b_cm_pw1, w_cm_dw, b_cm_dw, g_cm_ln, b_cm_ln, w_cm_pw2, b_cm_pw2, g_final):
    B, T, _ = x_prompt.shape
    NS, TS, _ = x_sample.shape
    assert D_FF % 512 == 0 and T % 512 == 0 and B + NS <= MOD_ROWS
    assert w_pool.shape[0] == w_sc_in.shape[0] == w_attn_in.shape[0] == w_cm_pw1.shape[0] == 1

    c_all = jnp.concatenate([c_prompt, c_sample, jnp.zeros((MOD_ROWS - B - NS, D), F32)], axis=0)
    mod_all = _mod_call(c_all, w_mod, b_mod)
    mods_p = [mod_all[i, 0:B].reshape(B, 1, 9 * D) for i in range(DEPTH)]
    mods_s = [mod_all[i, B:B + NS].reshape(NS, 1, 9 * D) for i in range(DEPTH)]

    wa = w_attn_in[0].astype(BF)
    W = dict(
        g_norm=g_norm, g_final=g_final.reshape(1, D),
        ffn32=(w_ffn_gate, w_ffn_up, w_ffn_down),
        ffn16={(0, 0): (w_ffn_gate[0, 0].astype(BF), w_ffn_up[0, 0].astype(BF), w_ffn_down[0, 0].astype(BF))},
        w_pool=w_pool[0].astype(BF), s_pool=s_pool,
        w_sc_in=w_sc_in[0].astype(BF), w_sc_conv=w_sc_conv[0], w_sc_out=w_sc_out[0].astype(BF),
        w_attn_main=wa, w_attn_tail=jnp.pad(wa[:, ATTN_MAIN:], ((0, 0), (0, LANE - ATTN_TAIL))),
        w_attn_out=w_attn_out[0].astype(BF),
        w_cm_pw1=w_cm_pw1[0].astype(BF), b_cm_pw1=b_cm_pw1, w_cm_dw=w_cm_dw[0], b_cm_dw=b_cm_dw,
        g_cm_ln=g_cm_ln, b_cm_ln=b_cm_ln, w_cm_pw2=w_cm_pw2[0].astype(BF), b_cm_pw2=b_cm_pw2,
    )

    y_p, st_p = _trunk(x_prompt, mods_p, 1, 512, 512, 512, W, 0, None)
    past = dict(pool=state_pool[0], sconv=state_sconv[0], cconv=state_cconv[0],
                k=cache_k[0].reshape(NS, -1, HEAD_DIM), v=cache_v[0].reshape(NS, -1, HEAD_DIM),
                kidx=cache_kidx[0])
    y_s, st_s = _trunk(x_sample, mods_s, NS, TS, 512, 512, W, cache_k.shape[2], past)

    def kv(a):
        return a.reshape(1, a.shape[0], a.shape[1], KV_HEADS, HEAD_DIM)

    return (y_p, y_s, st_p["pool"][None], st_s["pool"][None], st_p["sconv"][None], st_s["sconv"][None],
            kv(st_p["k"]), kv(st_s["k"]), kv(st_p["v"]), kv(st_s["v"]),
            st_p["kidx"][None], st_s["kidx"][None], st_p["cconv"][None], st_s["cconv"][None])
```

```python
import functools
import math

import jax
import jax.numpy as jnp
import numpy as np
from jax import lax
from jax.experimental import pallas as pl
from jax.experimental.pallas import tpu as pltpu

D = 2048
D_FF = 5632
DEPTH = 4
NORM_EPS = 1e-6
CHUNK = 64
POOL_WINDOWS = (2, 4, 8, 16)
POOL_GROUP = D // 4
POOL_PAD = 16
SCONV_WIDTH = 3
SCONV_PAD = 8
CCONV_WIDTH = 31
CCONV_PAD = 32
N_HEADS = 16
HEAD_DIM = 128
KV_HEADS = 4
HEAD_GROUP = N_HEADS // KV_HEADS
ROPE_DIM = HEAD_DIM // 4
ROPE_THETA = 500000.0
IDX_HEADS = 16
IDX_DIM = 64
IDX_ROPE_DIM = IDX_DIM // 4
IDX_SCALE = IDX_HEADS ** -0.5 * IDX_DIM ** -0.5
TOPK = 256
Q_W = N_HEADS * HEAD_DIM
KV_W = KV_HEADS * HEAD_DIM
IDXQ_W = IDX_HEADS * IDX_DIM
ATTN_MAIN = Q_W + 2 * KV_W + IDXQ_W
ATTN_TAIL = IDX_DIM + IDX_HEADS
LANE = 128
MOD_ROWS = 40
Q_SCALE = HEAD_DIM ** -0.5 * math.log2(math.e)
INT_MIN = -2 ** 31
NEG_BIG = float("-inf")

BF = jnp.bfloat16
F32 = jnp.float32


BF16_ROWS = 16
SUBLANES = 8
MXU_WIDTH = 256
CONV_CHUNK_ROWS = 64
CONV_CHUNK_LANES = 256


def _round_up(n, m):
    return -(-n // m) * m


def _cparams(sem, vmem_mib):
    return pltpu.CompilerParams(dimension_semantics=sem, vmem_limit_bytes=vmem_mib << 20)


def _rms(x, g):
    return x * lax.rsqrt(jnp.mean(x * x, axis=-1, keepdims=True) + NORM_EPS) * g


def _rms_mod(x, g, shift, scale):
    return _rms(x, g) * (1.0 + scale) + shift


def _silu(x):
    return x * jax.nn.sigmoid(x)


def _row_spec(S, R, W):
    return pl.BlockSpec((S, R, W), lambda b, t, n: (b, t, 0))


def _mod_spec(S, col):
    return pl.BlockSpec((S, 1, D), lambda b, t, n: (b, 0, col))


def _const_spec(shape):
    nd = len(shape)
    return pl.BlockSpec(shape, lambda b, t, n: (0,) * nd)


def _mod_kernel(c_ref, w_ref, b_ref, o_ref):
    a = _silu(c_ref[...]).astype(BF)
    o_ref[0] = jnp.dot(a, w_ref[0].astype(BF), preferred_element_type=F32) + b_ref[0]


def _mod_call(c_all, w_mod, b_mod):
    TN = 1024
    N = 9 * D
    return pl.pallas_call(
        _mod_kernel,
        grid=(DEPTH, N // TN),
        in_specs=[pl.BlockSpec((MOD_ROWS, D), lambda i, n: (0, 0)),
                  pl.BlockSpec((1, D, TN), lambda i, n: (i, 0, n)),
                  pl.BlockSpec((1, 1, TN), lambda i, n: (i, 0, n))],
        out_specs=pl.BlockSpec((1, MOD_ROWS, TN), lambda i, n: (i, 0, n)),
        out_shape=jax.ShapeDtypeStruct((DEPTH, MOD_ROWS, N), F32),
        compiler_params=_cparams(("arbitrary", "arbitrary"), 40),
        name="mod",
    )(c_all, w_mod, b_mod.reshape(DEPTH, 1, N))


def _ffn_kernel(*refs, S, R, final, cast_next):
    refs = list(refs)
    x_ref, sh_ref, sc_ref, gt_ref, g_ref, wg_ref, wu_ref, wd_ref = refs[:8]
    rest = refs[8:]
    if final:
        gf_ref = rest[0]
        rest = rest[1:]
    if cast_next:
        next32 = rest[:3]
        o_ref, *next16, h_scr = rest[3:]
    else:
        o_ref, h_scr = rest
    f = pl.program_id(2)
    M = S * R

    @pl.when(f == 0)
    def _():
        h = _rms_mod(x_ref[...], g_ref[...], sh_ref[...], sc_ref[...])
        h_scr[...] = h.reshape(M, D).astype(BF)
        o_ref[...] = jnp.zeros_like(o_ref)

    h = h_scr[...]
    acc = None
    for c0 in range(0, wg_ref.shape[1], MXU_WIDTH):
        cs = slice(c0, c0 + MXU_WIDTH)
        g = jnp.dot(h, wg_ref[:, cs], preferred_element_type=F32)
        u = jnp.dot(h, wu_ref[:, cs], preferred_element_type=F32)
        a = (_silu(g) * u).astype(BF)
        part = jnp.dot(a, wd_ref[cs, :], preferred_element_type=F32)
        acc = part if acc is None else acc + part
    o_ref[...] += acc.reshape(S, R, D)

    if cast_next:
        for src, dst in zip(next32, next16):
            dst[...] = src[...].astype(BF)

    @pl.when(f == pl.num_programs(2) - 1)
    def _():
        y = x_ref[...] + 0.5 * gt_ref[...] * o_ref[...]
        if final:
            y = _rms(y, gf_ref[...])
        o_ref[...] = y


def _ffn_call(x, mod, g, w16, sub, S, R, TF, gfin=None, next32=None):
    NB, TT, _ = x.shape
    final = gfin is not None
    nb, nt, nf = NB // S, TT // R, D_FF // TF
    in_specs = [_row_spec(S, R, D), _mod_spec(S, 3 * sub), _mod_spec(S, 3 * sub + 1),
                _mod_spec(S, 3 * sub + 2), _const_spec((1, D)),
                pl.BlockSpec((D, TF), lambda b, t, f: (0, f)),
                pl.BlockSpec((D, TF), lambda b, t, f: (0, f)),
                pl.BlockSpec((TF, D), lambda b, t, f: (f, 0))]
    args = [x, mod, mod, mod, g, *w16]
    if final:
        in_specs.append(_const_spec((1, D)))
        args.append(gfin)
    out_specs = [_row_spec(S, R, D)]
    out_shape = [jax.ShapeDtypeStruct(x.shape, F32)]
    if next32 is not None:
        (wg32, wu32, wd32), layer, which = next32
        up_rows, down_rows = D // (nb * nt), D_FF // (nb * nt * nf)
        assert up_rows * nb * nt == D and down_rows * nb * nt * nf == D_FF
        assert up_rows % BF16_ROWS == 0 and down_rows % BF16_ROWS == 0
        up_idx = lambda b, t, f: (b * nt + t, f)
        down_idx = lambda b, t, f: ((b * nt + t) * nf + f, 0)
        in_specs += [pl.BlockSpec((None, None, up_rows, TF), lambda b, t, f: (layer, which, *up_idx(b, t, f))),
                     pl.BlockSpec((None, None, up_rows, TF), lambda b, t, f: (layer, which, *up_idx(b, t, f))),
                     pl.BlockSpec((None, None, down_rows, D), lambda b, t, f: (layer, which, *down_idx(b, t, f)))]
        args += [wg32, wu32, wd32]
        out_specs += [pl.BlockSpec((up_rows, TF), up_idx), pl.BlockSpec((up_rows, TF), up_idx),
                      pl.BlockSpec((down_rows, D), down_idx)]
        out_shape += [jax.ShapeDtypeStruct((D, D_FF), BF), jax.ShapeDtypeStruct((D, D_FF), BF),
                      jax.ShapeDtypeStruct((D_FF, D), BF)]
    y, *w16_next = pl.pallas_call(
        functools.partial(_ffn_kernel, S=S, R=R, final=final, cast_next=next32 is not None),
        grid=(nb, nt, nf),
        in_specs=in_specs,
        out_specs=out_specs,
        out_shape=out_shape,
        scratch_shapes=[pltpu.VMEM((S * R, D), BF)],
        compiler_params=_cparams(("arbitrary", "arbitrary", "arbitrary"), 48),
        name="ffn",
    )(*args)
    return y, tuple(w16_next)


def _linres_kernel(*refs, S, R, ln, bias):
    refs = list(refs)
    a_ref, x_ref, gt_ref, w_ref = refs[:4]
    rest = refs[4:]
    if ln:
        gl_ref, bl_ref = rest[:2]
        rest = rest[2:]
    if bias:
        b_ref = rest[0]
        rest = rest[1:]
    (o_ref,) = rest
    a = a_ref[...]
    if ln:
        a = a.astype(F32)
        mu = jnp.mean(a, axis=-1, keepdims=True)
        ac = a - mu
        a = ac * lax.rsqrt(jnp.mean(ac * ac, axis=-1, keepdims=True) + NORM_EPS)
        a = _silu(a * gl_ref[...] + bl_ref[...])
    a = a.reshape(S * R, a.shape[-1]).astype(BF)
    y = jnp.dot(a, w_ref[...], preferred_element_type=F32)
    if bias:
        y = y + b_ref[...]
    o_ref[...] = x_ref[...] + gt_ref[...] * y.reshape(o_ref.shape)


def _linres_call(a, x, mod, gate_col, w, S, R, ln=None, bias=None):
    NB, TT, K = a.shape
    in_specs = [_row_spec(S, R, K), _row_spec(S, R, D), _mod_spec(S, gate_col),
                pl.BlockSpec((K, D), lambda b, t, n: (0, 0), pipeline_mode=pl.Buffered(1))]
    args = [a, x, mod, w]
    if ln is not None:
        in_specs += [_const_spec((1, K)), _const_spec((1, K))]
        args += list(ln)
    if bias is not None:
        in_specs.append(_const_spec((1, D)))
        args.append(bias)
    return pl.pallas_call(
        functools.partial(_linres_kernel, S=S, R=R, ln=ln is not None, bias=bias is not None),
        grid=(NB // S, TT // R, 1),
        in_specs=in_specs,
        out_specs=_row_spec(S, R, D),
        out_shape=jax.ShapeDtypeStruct(x.shape, F32),
        compiler_params=_cparams(("arbitrary", "arbitrary", "arbitrary"), 56),
        name="linres",
    )(*args)


def _pool_kernel(*refs, R, has_prefix, n_valid_prefix):
    if has_prefix:
        (x_ref, sh_ref, sc_ref, gt_ref, g_ref, wp_ref, sp_ref, pre_ref,
         o_ref, st_ref, hext) = refs
    else:
        (x_ref, sh_ref, sc_ref, gt_ref, g_ref, wp_ref, sp_ref,
         o_ref, st_ref, hext, carry) = refs
    t = pl.program_id(1)
    x = x_ref[0]
    h = _rms_mod(x, g_ref[...], sh_ref[0], sc_ref[0])
    if has_prefix:
        hext[0:POOL_PAD, :] = pre_ref[0]
    else:
        @pl.when(t == 0)
        def _():
            hext[0:POOL_PAD, :] = jnp.zeros((POOL_PAD, D), F32)

        @pl.when(t > 0)
        def _():
            hext[0:POOL_PAD, :] = carry[...]

        carry[...] = h[R - POOL_PAD:R, :]
    hext[POOL_PAD:POOL_PAD + R, :] = h
    st_ref[0] = h[R - POOL_PAD:R, :]
    tpos = t * R + lax.broadcasted_iota(jnp.int32, (R, 1), 0)
    gate = gt_ref[0]
    for gi, w in enumerate(POOL_WINDOWS):
        lo = gi * POOL_GROUP
        hi = lo + POOL_GROUP
        rc = min(R, CONV_CHUNK_ROWS)
        wins = []
        for r0 in range(0, R, rc):
            acc = hext[POOL_PAD + r0:POOL_PAD + r0 + rc, lo:hi]
            for j in range(1, w):
                acc = acc + hext[POOL_PAD + r0 - j:POOL_PAD + r0 - j + rc, lo:hi]
            wins.append(acc)
        win = jnp.concatenate(wins, axis=0) if len(wins) > 1 else wins[0]
        cnt = jnp.minimum(tpos + 1 + n_valid_prefix, w).astype(F32)
        pooled = (win / cnt - h[:, lo:hi]).astype(BF)
        y = jnp.dot(pooled, wp_ref[gi], preferred_element_type=F32) * sp_ref[:, lo:hi]
        o_ref[0, :, lo:hi] = x[:, lo:hi] + gate[:, lo:hi] * y


def _pool_call(x, mod, g, w_pool, s_pool, R, prefix=None):
    NB, TT, _ = x.shape
    has_prefix = prefix is not None
    in_specs = [_row_spec(1, R, D), _mod_spec(1, 3), _mod_spec(1, 4), _mod_spec(1, 5),
                _const_spec((1, D)), _const_spec((4, POOL_GROUP, POOL_GROUP)), _const_spec((1, D))]
    args = [x, mod, mod, mod, g, w_pool, s_pool]
    scratch = [pltpu.VMEM((R + POOL_PAD, D), F32)]
    if has_prefix:
        in_specs.append(pl.BlockSpec((1, POOL_PAD, D), lambda b, t, n: (b, 0, 0)))
        args.append(prefix)
    else:
        scratch.append(pltpu.VMEM((POOL_PAD, D), F32))
    return pl.pallas_call(
        functools.partial(_pool_kernel, R=R, has_prefix=has_prefix,
                          n_valid_prefix=POOL_PAD - 1 if has_prefix else 0),
        grid=(NB, TT // R, 1),
        in_specs=in_specs,
        out_specs=[_row_spec(1, R, D), pl.BlockSpec((1, POOL_PAD, D), lambda b, t, n: (b, 0, 0))],
        out_shape=[jax.ShapeDtypeStruct(x.shape, F32), jax.ShapeDtypeStruct((NB, POOL_PAD, D), F32)],
        scratch_shapes=scratch,
        compiler_params=_cparams(("arbitrary", "arbitrary", "arbitrary"), 40),
        name="pool",
    )(*args)


def _conv_kernel(*refs, S, R, TN, mode, has_prefix):
    refs = list(refs)
    x_ref, sh_ref, sc_ref, g_ref = refs[:4]
    rest = refs[4:]
    nproj = 3 if mode == "sconv" else 2
    w_refs = rest[:nproj]
    rest = rest[nproj:]
    if mode == "cconv":
        pb_refs = rest[:2]
        wc_ref, bdw_ref = rest[2:4]
        rest = rest[4:]
        width, pad = CCONV_WIDTH, CCONV_PAD
    else:
        wc_ref = rest[0]
        rest = rest[1:]
        width, pad = SCONV_WIDTH, SCONV_PAD
    if has_prefix:
        pre_ref = rest[0]
        z_ref, st_ref, h_scr, uext, vbuf = rest[1:]
    else:
        z_ref, st_ref, h_scr, uext, vbuf, carry = rest
    t = pl.program_id(1)
    n = pl.program_id(2)
    M = S * R

    @pl.when(n == 0)
    def _():
        h = _rms_mod(x_ref[...], g_ref[...], sh_ref[...], sc_ref[...])
        h_scr[...] = h.reshape(M, D).astype(BF)

    h = h_scr[...]
    proj = [jnp.dot(h, w_ref[...], preferred_element_type=F32) for w_ref in w_refs]
    if mode == "sconv":
        bq, cq, vq = proj
        u = cq * vq
    else:
        a = proj[0] + pb_refs[0][...]
        gq = proj[1] + pb_refs[1][...]
        u = a * jax.nn.sigmoid(gq)
    u3 = u.reshape(S, R, TN)
    if has_prefix:
        uext[:, 0:pad, :] = pre_ref[...]
    else:
        @pl.when(t == 0)
        def _():
            uext[:, 0:pad, :] = jnp.zeros((S, pad, TN), F32)

        @pl.when(t > 0)
        def _():
            uext[:, 0:pad, :] = carry[n]
    uext[:, pad:pad + R, :] = u3
    tail = uext[:, R:R + pad, :]
    if not has_prefix:
        carry[n] = tail
    st_ref[...] = tail
    wc = wc_ref[...]
    if mode == "sconv":
        bq3 = bq.reshape(S, R, TN)
    base = pad - (width - 1)
    rc = min(R, CONV_CHUNK_ROWS)
    sc_ = max(1, CONV_CHUNK_ROWS // rc)
    for s0 in range(0, S, sc_):
        for r0 in range(0, R, rc):
            for l0 in range(0, TN, CONV_CHUNK_LANES):
                ls = slice(l0, l0 + CONV_CHUNK_LANES)
                acc = None
                for sub in range(min(SUBLANES, width)):
                    taps = range(sub, width, SUBLANES)
                    start = r0 + base + sub
                    span = rc + taps[-1] - sub
                    vbuf[:, 0:span, :] = uext[s0:s0 + sc_, start:start + span, ls]
                    for j in taps:
                        term = wc[j:j + 1, ls] * vbuf[:, j - sub:j - sub + rc, :]
                        acc = term if acc is None else acc + term
                if mode == "sconv":
                    z_ref[s0:s0 + sc_, r0:r0 + rc, ls] = (bq3[s0:s0 + sc_, r0:r0 + rc, ls] * acc).astype(z_ref.dtype)
                else:
                    z_ref[s0:s0 + sc_, r0:r0 + rc, ls] = acc + bdw_ref[:, ls]


def _conv_call(x, mod, g, w, mode, S, R, TN, w_conv, pbias=None, b_dw=None, prefix=None):
    NB, TT, _ = x.shape
    has_prefix = prefix is not None
    nproj = 3 if mode == "sconv" else 2
    width, pad = (SCONV_WIDTH, SCONV_PAD) if mode == "sconv" else (CCONV_WIDTH, CCONV_PAD)
    nN = D // TN
    in_specs = [_row_spec(S, R, D), _mod_spec(S, 3), _mod_spec(S, 4), _const_spec((1, D))]
    args = [x, mod, mod, g]
    for k in range(nproj):
        in_specs.append(pl.BlockSpec((D, TN), lambda b, t, n, k=k: (0, k * nN + n)))
        args.append(w)
    if mode == "cconv":
        for k in range(2):
            in_specs.append(pl.BlockSpec((1, TN), lambda b, t, n, k=k: (0, k * nN + n)))
            args.append(pbias)
    in_specs.append(pl.BlockSpec((width, TN), lambda b, t, n: (0, n)))
    args.append(w_conv)
    if mode == "cconv":
        in_specs.append(pl.BlockSpec((1, TN), lambda b, t, n: (0, n)))
        args.append(b_dw)
    rc = min(R, CONV_CHUNK_ROWS)
    scratch = [pltpu.VMEM((S * R, D), BF), pltpu.VMEM((S, R + pad, TN), F32),
               pltpu.VMEM((max(1, CONV_CHUNK_ROWS // rc), rc + _round_up(width, SUBLANES), CONV_CHUNK_LANES), F32)]
    if has_prefix:
        in_specs.append(pl.BlockSpec((S, pad, TN), lambda b, t, n: (b, 0, n)))
        args.append(prefix)
    else:
        scratch.append(pltpu.VMEM((nN, S, pad, TN), F32))
    zdt = BF if mode == "sconv" else F32
    z, tails = pl.pallas_call(
        functools.partial(_conv_kernel, S=S, R=R, TN=TN, mode=mode, has_prefix=has_prefix),
        grid=(NB // S, TT // R, nN),
        in_specs=in_specs,
        out_specs=[pl.BlockSpec((S, R, TN), lambda b, t, n: (b, t, n)),
                   pl.BlockSpec((S, pad, TN), lambda b, t, n: (b, t, n))],
        out_shape=[jax.ShapeDtypeStruct(x.shape, zdt),
                   jax.ShapeDtypeStruct((NB, (TT // R) * pad, D), F32)],
        scratch_shapes=scratch,
        compiler_params=_cparams(("arbitrary", "arbitrary", "arbitrary"), 40),
        name=mode,
    )(*args)
    return z, tails[:, -(width - 1):]


def _rope_lanes(x, cos, sin, period, half):
    lane = lax.broadcasted_iota(jnp.int32, x.shape, x.ndim - 1) % period
    width = x.shape[-1]
    partner = jnp.where(lane < half, pltpu.roll(x, width - half, x.ndim - 1),
                        pltpu.roll(x, half, x.ndim - 1))
    return x * cos + partner * sin


def _aproj_kernel(x_ref, sh_ref, sc_ref, g_ref, w_ref, wt_ref,
                  ch_ref, sh_h_ref, ci_ref, si_ref, ct_ref, st_ref,
                  q_ref, k32_ref, kb_ref, v32_ref, vb_ref, qi_ref, ki32_ref, kib_ref, wi_ref,
                  h_scr, *, S, R):
    n = pl.program_id(2)
    M = S * R
    TN = 4 * HEAD_DIM

    @pl.when(n == 0)
    def _():
        h = _rms_mod(x_ref[...], g_ref[...], sh_ref[...], sc_ref[...])
        h_scr[...] = h.reshape(M, D).astype(BF)

    def heads_rope(p):
        cos, sin = ch_ref[...], sh_h_ref[...]
        return jnp.concatenate(
            [_rope_lanes(p[:, i * HEAD_DIM:(i + 1) * HEAD_DIM], cos, sin, HEAD_DIM, ROPE_DIM // 2)
             for i in range(TN // HEAD_DIM)], axis=-1)

    @pl.when(n < 4)
    def _():
        p = jnp.dot(h_scr[...], w_ref[...], preferred_element_type=F32)
        q = heads_rope(p) * Q_SCALE
        q_ref[...] = q.reshape(S, R, TN).astype(BF)

    @pl.when(n == 4)
    def _():
        p = jnp.dot(h_scr[...], w_ref[...], preferred_element_type=F32)
        k = heads_rope(p).reshape(S, R, TN)
        k32_ref[...] = k
        kb_ref[...] = k.astype(BF)

    @pl.when(n == 5)
    def _():
        v = jnp.dot(h_scr[...], w_ref[...], preferred_element_type=F32).reshape(S, R, TN)
        v32_ref[...] = v
        vb_ref[...] = v.astype(BF)

    @pl.when((n == 6) | (n == 7))
    def _():
        p = jnp.dot(h_scr[...], w_ref[...], preferred_element_type=F32)
        cos, sin = ci_ref[...], si_ref[...]
        qi = jnp.concatenate(
            [_rope_lanes(p[:, i * LANE:(i + 1) * LANE], cos, sin, IDX_DIM, IDX_ROPE_DIM // 2)
             for i in range(TN // LANE)], axis=-1)
        qi_ref[...] = qi.reshape(S, R, TN).astype(BF)

    @pl.when(n == 8)
    def _():
        p = jnp.dot(h_scr[...], wt_ref[...], preferred_element_type=F32)
        tl = _rope_lanes(p, ct_ref[...], st_ref[...], LANE, IDX_ROPE_DIM // 2)
        ki = tl[:, 0:IDX_DIM].reshape(S, R, IDX_DIM)
        ki32_ref[...] = ki
        kib_ref[...] = ki.astype(BF)
        wi_ref[...] = (tl[:, IDX_DIM:IDX_DIM + IDX_HEADS] * IDX_SCALE).reshape(S, R, IDX_HEADS)


def _aproj_call(x, mod, g, w_main, w_tail, tabs, S, R):
    NB, TT, _ = x.shape
    M = S * R
    TN = 4 * HEAD_DIM
    tab_spec = pl.BlockSpec((M, LANE), lambda b, t, n: (t, 0))

    def out(width, dt, idx=lambda b, t, n: (b, t, 0)):
        return (pl.BlockSpec((S, R, min(width, TN)), idx), jax.ShapeDtypeStruct((NB, TT, width), dt))

    outs = [out(Q_W, BF, lambda b, t, n: (b, t, jnp.minimum(n, 3))),
            out(KV_W, F32), out(KV_W, BF), out(KV_W, F32), out(KV_W, BF),
            out(IDXQ_W, BF, lambda b, t, n: (b, t, jnp.clip(n - 6, 0, 1))),
            out(IDX_DIM, F32), out(IDX_DIM, BF), out(IDX_HEADS, F32)]
    return pl.pallas_call(
        functools.partial(_aproj_kernel, S=S, R=R),
        grid=(NB // S, TT // R, 9),
        in_specs=[_row_spec(S, R, D), _mod_spec(S, 3), _mod_spec(S, 4), _const_spec((1, D)),
                  pl.BlockSpec((D, TN), lambda b, t, n: (0, jnp.minimum(n, ATTN_MAIN // TN - 1))),
                  _const_spec((D, LANE))] + [tab_spec] * 6,
        out_specs=[o[0] for o in outs],
        out_shape=[o[1] for o in outs],
        scratch_shapes=[pltpu.VMEM((M, D), BF)],
        compiler_params=_cparams(("arbitrary", "arbitrary", "arbitrary"), 40),
        name="attn_proj",
    )(x, mod, mod, g, w_main, w_tail, *tabs)


def _rope_tables(pos, reps):
    def tab(rot_dim, period, lanes_used):
        half = rot_dim // 2
        inv = jnp.exp(-math.log(ROPE_THETA) * jnp.arange(half, dtype=F32) * (2.0 / rot_dim))
        ang = pos.astype(F32)[:, None] * inv[None, :]
        cos, sin = jnp.cos(ang), jnp.sin(ang)
        n = pos.shape[0]
        one = jnp.ones((n, period - rot_dim), F32)
        zero = jnp.zeros((n, period - rot_dim), F32)
        c = jnp.concatenate([cos, cos, one], axis=1)
        s = jnp.concatenate([-sin, sin, zero], axis=1)
        c = jnp.tile(c, (1, lanes_used // period))
        s = jnp.tile(s, (1, lanes_used // period))
        if lanes_used < LANE:
            c = jnp.concatenate([c, jnp.ones((n, LANE - lanes_used), F32)], axis=1)
            s = jnp.concatenate([s, jnp.zeros((n, LANE - lanes_used), F32)], axis=1)
        return [jnp.tile(c, (reps, 1)), jnp.tile(s, (reps, 1))]

    return tab(ROPE_DIM, HEAD_DIM, LANE) + tab(IDX_ROPE_DIM, IDX_DIM, LANE) + tab(IDX_ROPE_DIM, IDX_DIM, IDX_DIM)


def _topk_threshold(count, rewrite, rows, n_top, pos_bits, count_hi=None):
    n_top = float(n_top)

    def bit_step(i, carry, nbits, count_ge):
        thr, n_ge = carry
        cand = jnp.where(i == 0, jnp.zeros_like(thr), thr | jnp.left_shift(jnp.int32(1), nbits - 1 - i))
        n_cand = count_ge(cand)
        take = n_cand >= n_top
        return jnp.where(take, cand, thr), jnp.where(take, n_cand, n_ge)

    n_ge = jnp.zeros((rows, 1), F32)
    count32 = lambda cand: count(lambda k, p: k >= cand)
    if count_hi is None:
        first, thr = 0, jnp.full((rows, 1), INT_MIN, jnp.int32)
    else:
        first = 16
        thr, n_ge = lax.fori_loop(0, first, functools.partial(bit_step, nbits=16, count_ge=count_hi),
                                  (jnp.full((rows, 1), -2 ** 15, jnp.int32), n_ge))
        thr = jnp.left_shift(thr, 16)
    thr, n_ge = lax.fori_loop(first, 32, functools.partial(bit_step, nbits=32, count_ge=count32), (thr, n_ge))
    excess = n_ge > n_top

    @pl.when(jnp.max(jnp.where(excess, 1, 0)) > 0)
    def _():
        quota = n_top - count(lambda k, p: k > thr)

        def pos_body(i, cut):
            cand = cut | jnp.left_shift(jnp.int32(1), pos_bits - 1 - i)
            return jnp.where(count(lambda k, p: (k == thr) & (p < cand)) < quota, cand, cut)

        cut = lax.fori_loop(0, pos_bits, pos_body, jnp.zeros((rows, 1), jnp.int32))
        rewrite(lambda k, p: jnp.where((k == thr) & (p > cut), jnp.int32(INT_MIN), k))

    return jnp.maximum(thr, jnp.int32(INT_MIN + 1))


def _acore_kernel(q_ref, qi_ref, wi_ref, ki_ref, kb_ref, vb_ref, o_ref,
                  keys, keys_hi, m_scr, l_scr, acc_scr, *, TQ, KT, NT, pos0, l_valid, n_top):
    j = pl.program_id(1)
    q0 = pos0 + j * TQ
    last_key = ((q0 + TQ - 1) // CHUNK) * CHUNK + CHUNK - 1
    nt = jnp.minimum(last_key // KT + 1, NT)
    qchunk = (q0 + lax.broadcasted_iota(jnp.int32, (TQ, 1), 0)) // CHUNK

    qi = qi_ref[0]
    qst = jnp.concatenate([qi[:, h * IDX_DIM:(h + 1) * IDX_DIM] for h in range(IDX_HEADS)], axis=0)
    wi = wi_ref[0]
    wst = jnp.concatenate([wi[:, h:h + 1] for h in range(IDX_HEADS)], axis=0)

    def score_body(kt, carry):
        start = pl.multiple_of(kt * KT, KT)
        key = _score_keys(qst, wst, ki_ref[0, pl.ds(start, KT), :], TQ)
        kpos = start + lax.broadcasted_iota(jnp.int32, (1, KT), 1)
        adm = (kpos // CHUNK <= qchunk) & (kpos < l_valid)
        key = jnp.where(adm, key, jnp.int32(INT_MIN))
        keys[kt] = key
        keys_hi[kt] = jnp.right_shift(key, 16).astype(jnp.int16)
        return carry

    lax.fori_loop(0, nt, score_body, 0)

    def count_hi(cand):
        cand16 = cand.astype(jnp.int16)

        def body(kt, acc):
            hit = jnp.where(keys_hi[kt] >= cand16, jnp.int16(1), jnp.int16(0))
            for c in range(KT // LANE):
                acc = acc + hit[:, c * LANE:(c + 1) * LANE]
            return acc
        acc = lax.fori_loop(0, nt, body, jnp.zeros((TQ, LANE), jnp.int16))
        return jnp.sum(acc.astype(F32), axis=-1, keepdims=True)

    def tile_pos(kt):
        return kt * KT + lax.broadcasted_iota(jnp.int32, (1, KT), 1)

    def count(pred):
        def body(kt, acc):
            hit = jnp.where(pred(keys[kt], tile_pos(kt)), 1.0, 0.0)
            for c in range(KT // LANE):
                acc = acc + hit[:, c * LANE:(c + 1) * LANE]
            return acc
        acc = lax.fori_loop(0, nt, body, jnp.zeros((TQ, LANE), F32))
        return jnp.sum(acc, axis=-1, keepdims=True)

    def rewrite(fn):
        def body(kt, carry):
            keys[kt] = fn(keys[kt], tile_pos(kt))
            return carry
        lax.fori_loop(0, nt, body, 0)

    thr = _topk_threshold(count, rewrite, TQ, n_top, (NT * KT).bit_length(), count_hi)

    q = q_ref[0]
    G = HEAD_GROUP
    qgs = [jnp.concatenate([q[:, (G * g + r) * HEAD_DIM:(G * g + r + 1) * HEAD_DIM] for r in range(G)],
                           axis=0) for g in range(KV_HEADS)]

    def scores(kt, g):
        start = pl.multiple_of(kt * KT, KT)
        kt_ = kb_ref[0, pl.ds(start, KT), g * HEAD_DIM:(g + 1) * HEAD_DIM]
        s = lax.dot_general(qgs[g], kt_, (((1,), (1,)), ((), ())), preferred_element_type=F32)
        return s.reshape(G, TQ, KT)

    def lane_fold(x, op):
        part = x[:, 0:LANE]
        for c in range(1, KT // LANE):
            part = op(part, x[:, c * LANE:(c + 1) * LANE])
        return part

    m_scr[...] = jnp.full((KV_HEADS, G * TQ, LANE), NEG_BIG, F32)

    def max_body(kt, carry):
        sel = (keys[kt] >= thr)[None]
        for g in range(KV_HEADS):
            s = jnp.where(sel, scores(kt, g), NEG_BIG).reshape(G * TQ, KT)
            m_scr[g] = jnp.maximum(m_scr[g], lane_fold(s, jnp.maximum))
        return carry

    lax.fori_loop(0, nt, max_body, 0)
    ms = [jnp.max(m_scr[g], axis=-1, keepdims=True).reshape(G, TQ, 1) for g in range(KV_HEADS)]
    l_scr[...] = jnp.zeros((KV_HEADS, G * TQ, LANE), F32)
    acc_scr[...] = jnp.zeros((KV_HEADS, G * TQ, HEAD_DIM), F32)

    def pv_body(kt, carry):
        start = pl.multiple_of(kt * KT, KT)
        sel = (keys[kt] >= thr)[None]
        for g in range(KV_HEADS):
            vt_ = vb_ref[0, pl.ds(start, KT), g * HEAD_DIM:(g + 1) * HEAD_DIM]
            p = jnp.where(sel, jnp.exp2(scores(kt, g) - ms[g]), 0.0).reshape(G * TQ, KT)
            l_scr[g] += lane_fold(p, jnp.add)
            acc_scr[g] += jnp.dot(p.astype(BF), vt_, preferred_element_type=F32)
        return carry

    lax.fori_loop(0, nt, pv_body, 0)
    for g in range(KV_HEADS):
        og = acc_scr[g] / jnp.sum(l_scr[g], axis=-1, keepdims=True)
        for r in range(G):
            o_ref[0, :, (G * g + r) * HEAD_DIM:(G * g + r + 1) * HEAD_DIM] = og[r * TQ:(r + 1) * TQ].astype(BF)


def _acore_call(q, qi, wi, kib, kb, vb, TQ, KT, pos0, l_valid):
    NB, TT, _ = q.shape
    LP = kb.shape[1]
    NT = LP // KT
    n_top = min(TOPK, l_valid // 4)
    res = lambda w: pl.BlockSpec((1, LP, w), lambda b, j: (b, 0, 0), pipeline_mode=pl.Buffered(1))
    blk = lambda w: pl.BlockSpec((1, TQ, w), lambda b, j: (b, j, 0))
    G = HEAD_GROUP
    return pl.pallas_call(
        functools.partial(_acore_kernel, TQ=TQ, KT=KT, NT=NT, pos0=pos0, l_valid=l_valid, n_top=n_top),
        grid=(NB, TT // TQ),
        in_specs=[blk(Q_W), blk(IDXQ_W), blk(IDX_HEADS), res(IDX_DIM), res(KV_W), res(KV_W)],
        out_specs=blk(Q_W),
        out_shape=jax.ShapeDtypeStruct((NB, TT, Q_W), BF),
        scratch_shapes=[pltpu.VMEM((NT, TQ, KT), jnp.int32), pltpu.VMEM((NT, TQ, KT), jnp.int16),
                        pltpu.VMEM((KV_HEADS, G * TQ, LANE), F32),
                        pltpu.VMEM((KV_HEADS, G * TQ, LANE), F32),
                        pltpu.VMEM((KV_HEADS, G * TQ, HEAD_DIM), F32)],
        compiler_params=_cparams(("arbitrary", "arbitrary"), 56),
        name="attn_core",
    )(q, qi, wi, kib, kb, vb)


def _score_keys(qst, wst, kmat, TQ):
    s = lax.dot_general(qst, kmat, (((1,), (1,)), ((), ())), preferred_element_type=F32)
    s = jnp.maximum(s, 0.0) * wst
    sc = jnp.sum(s.reshape(IDX_HEADS, TQ, kmat.shape[0]), axis=0) + 0.0
    bits = lax.bitcast_convert_type(sc, jnp.int32)
    return jnp.where(bits < 0, bits ^ jnp.int32(0x7FFFFFFF), bits)


def _dec_index_kernel(qi_ref, wi_ref, cki_ref, kin_ref, keys_ref, thr_ref, kscr, *, SB, TQ, LC, KT, pos0, n_top):
    rows = SB * TQ
    width = LC + LANE
    qchunk = (pos0 + lax.broadcasted_iota(jnp.int32, (TQ, 1), 0)) // CHUNK
    for s in range(SB):
        qi = qi_ref[s]
        qst = jnp.concatenate([qi[:, h * IDX_DIM:(h + 1) * IDX_DIM] for h in range(IDX_HEADS)], axis=0)
        wi = wi_ref[s]
        wst = jnp.concatenate([wi[:, h:h + 1] for h in range(IDX_HEADS)], axis=0)
        for c in range(LC // KT):
            kit = cki_ref[s, c * KT:(c + 1) * KT, :].astype(BF)
            kpos = c * KT + lax.broadcasted_iota(jnp.int32, (1, KT), 1)
            key = _score_keys(qst, wst, kit, TQ)
            kscr[s * TQ:(s + 1) * TQ, c * KT:(c + 1) * KT] = jnp.where(kpos // CHUNK <= qchunk, key,
                                                                         jnp.int32(INT_MIN))
        kin = jnp.concatenate([kin_ref[s], jnp.zeros((LANE - TQ, IDX_DIM), BF)], axis=0)
        lane = lax.broadcasted_iota(jnp.int32, (1, LANE), 1)
        key = _score_keys(qst, wst, kin, TQ)
        adm = ((LC + lane) // CHUNK <= qchunk) & (lane < TQ)
        kscr[s * TQ:(s + 1) * TQ, LC:width] = jnp.where(adm, key, jnp.int32(INT_MIN))

    def chunk_pos(c):
        return c * LANE + lax.broadcasted_iota(jnp.int32, (1, LANE), 1)

    def count(pred):
        acc = jnp.zeros((rows, LANE), F32)
        for c in range(width // LANE):
            acc = acc + jnp.where(pred(kscr[:, c * LANE:(c + 1) * LANE], chunk_pos(c)), 1.0, 0.0)
        return jnp.sum(acc, axis=-1, keepdims=True)

    def rewrite(fn):
        for c in range(width // LANE):
            kscr[:, c * LANE:(c + 1) * LANE] = fn(kscr[:, c * LANE:(c + 1) * LANE], chunk_pos(c))

    thr = _topk_threshold(count, rewrite, rows, n_top, width.bit_length())
    thr_ref[...] = thr.reshape(SB, TQ, 1)
    keys_ref[...] = kscr[...].reshape(SB, TQ, width)


def _dec_core_kernel(q_ref, keys_ref, thr_ref, ck_ref, cv_ref, kn_ref, vn_ref, o_ref, *, TQ, LC):
    G = HEAD_GROUP
    q = q_ref[0]
    sel = keys_ref[0] >= thr_ref[0]
    sel_c, sel_n = sel[None, :, 0:LC], sel[None, :, LC:LC + LANE]
    kn_all, vn_all = kn_ref[0], vn_ref[0]
    pad_rows = jnp.zeros((LANE - TQ, HEAD_DIM), BF)
    nt_dims = (((1,), (1,)), ((), ()))
    for g in range(KV_HEADS):
        hs = slice(g * HEAD_DIM, (g + 1) * HEAD_DIM)
        qg = jnp.concatenate([q[:, (G * g + r) * HEAD_DIM:(G * g + r + 1) * HEAD_DIM] for r in range(G)], axis=0)
        kc = ck_ref[0, pl.ds(g, LC, stride=KV_HEADS), :].astype(BF)
        vc = cv_ref[0, pl.ds(g, LC, stride=KV_HEADS), :].astype(BF)
        kn = jnp.concatenate([kn_all[:, hs], pad_rows], axis=0)
        vn = jnp.concatenate([vn_all[:, hs], pad_rows], axis=0)
        sc = lax.dot_general(qg, kc, nt_dims, preferred_element_type=F32).reshape(G, TQ, LC)
        sn = lax.dot_general(qg, kn, nt_dims, preferred_element_type=F32).reshape(G, TQ, LANE)
        sc = jnp.where(sel_c, sc, NEG_BIG).reshape(G * TQ, LC)
        sn = jnp.where(sel_n, sn, NEG_BIG).reshape(G * TQ, LANE)
        m = jnp.maximum(jnp.max(sc, axis=-1, keepdims=True), jnp.max(sn, axis=-1, keepdims=True))
        pc = jnp.exp2(sc - m)
        pn = jnp.exp2(sn - m)
        l = jnp.sum(pc, axis=-1, keepdims=True) + jnp.sum(pn, axis=-1, keepdims=True)
        og = (jnp.dot(pc.astype(BF), vc, preferred_element_type=F32)
              + jnp.dot(pn.astype(BF), vn, preferred_element_type=F32)) / l
        for r in range(G):
            o_ref[0, :, (G * g + r) * HEAD_DIM:(G * g + r + 1) * HEAD_DIM] = og[r * TQ:(r + 1) * TQ].astype(BF)


def _dec_attn_call(q, qi, wi, kib, kb, vb, cache_k, cache_v, cache_kidx, pos0):
    NS, TQ, _ = q.shape
    LC = cache_kidx.shape[1]
    KT, SB = 512, 8
    assert NS % SB == 0 and LC % KT == 0 and TQ <= LANE
    n_top = min(TOPK, (LC + TQ) // 4)
    width = LC + LANE
    blk = lambda n, w: pl.BlockSpec((n, TQ, w), lambda b: (b, 0, 0))
    keys, thr = pl.pallas_call(
        functools.partial(_dec_index_kernel, SB=SB, TQ=TQ, LC=LC, KT=KT, pos0=pos0, n_top=n_top),
        grid=(NS // SB,),
        in_specs=[blk(SB, IDXQ_W), blk(SB, IDX_HEADS),
                  pl.BlockSpec((SB, LC, IDX_DIM), lambda b: (b, 0, 0)), blk(SB, IDX_DIM)],
        out_specs=[blk(SB, width), blk(SB, 1)],
        out_shape=[jax.ShapeDtypeStruct((NS, TQ, width), jnp.int32),
                   jax.ShapeDtypeStruct((NS, TQ, 1), jnp.int32)],
        scratch_shapes=[pltpu.VMEM((SB * TQ, width), jnp.int32)],
        compiler_params=_cparams(("arbitrary",), 48),
        name="attn_dec_index",
    )(qi, wi, cache_kidx, kib)
    cache_spec = pl.BlockSpec((1, LC * KV_HEADS, HEAD_DIM), lambda b: (b, 0, 0))
    return pl.pallas_call(
        functools.partial(_dec_core_kernel, TQ=TQ, LC=LC),
        grid=(NS,),
        in_specs=[blk(1, Q_W), blk(1, width), blk(1, 1), cache_spec, cache_spec, blk(1, KV_W), blk(1, KV_W)],
        out_specs=blk(1, Q_W),
        out_shape=jax.ShapeDtypeStruct((NS, TQ, Q_W), BF),
        compiler_params=_cparams(("arbitrary",), 48),
        name="attn_dec_core",
    )(q, keys, thr, cache_k, cache_v, kb, vb)


def _trunk(x, mods, S, R, TF, TN, W, pos0, past):
    NB, TT, _ = x.shape
    fresh = past is None
    states = {}
    ffn16 = W["ffn16"]

    def ffn(x, mod, g, layer, which, gfin=None):
        nxt = (layer, which + 1) if which == 0 else (layer + 1, 0)
        make_next = fresh and nxt[0] < DEPTH
        y, w16_next = _ffn_call(x, mod, g, ffn16[layer, which], 2 * which, S, R, TF, gfin,
                                (W["ffn32"], *nxt) if make_next else None)
        if make_next:
            ffn16[nxt] = w16_next
        return y

    for i in range(DEPTH):
        kind = i % 4
        mod = mods[i]
        gn = W["g_norm"][i]
        x = ffn(x, mod, gn[0:1], i, 0)
        if kind == 0:
            prefix = None if fresh else jnp.pad(past["pool"], ((0, 0), (1, 0), (0, 0)))
            x, st = _pool_call(x, mod, gn[1:2], W["w_pool"], W["s_pool"], R if fresh else TT, prefix)
            states["pool"] = st[:, 1:]
        elif kind == 1:
            prefix = None if fresh else jnp.pad(past["sconv"], ((0, 0), (SCONV_PAD - 2, 0), (0, 0)))
            z, st = _conv_call(x, mod, gn[1:2], W["w_sc_in"], "sconv", S, R, TN, W["w_sc_conv"],
                               prefix=prefix)
            states["sconv"] = st
            x = _linres_call(z, x, mod, 5, W["w_sc_out"], S, R)
        elif kind == 2:
            reps = 1 if fresh else NB
            tabs = _rope_tables(pos0 + jnp.arange(TT), reps)
            q, k32, kb, v32, vb, qi, ki32, kib, wi = _aproj_call(
                x, mod, gn[1:2], W["w_attn_main"], W["w_attn_tail"], tabs, S, R)
            states["k"], states["v"], states["kidx"] = k32, v32, ki32
            if fresh:
                o = _acore_call(q, qi, wi, kib, kb, vb, 256, 512, pos0, TT)
            else:
                o = _dec_attn_call(q, qi, wi, kib, kb, vb, past["k"], past["v"], past["kidx"], pos0)
            x = _linres_call(o, x, mod, 5, W["w_attn_out"], S, R)
        else:
            prefix = None if fresh else jnp.pad(past["cconv"], ((0, 0), (CCONV_PAD - 30, 0), (0, 0)))
            cv, st = _conv_call(x, mod, gn[1:2], W["w_cm_pw1"], "cconv", S, R, TN, W["w_cm_dw"],
                                pbias=W["b_cm_pw1"], b_dw=W["b_cm_dw"], prefix=prefix)
            states["cconv"] = st
            x = _linres_call(cv, x, mod, 5, W["w_cm_pw2"], S, R,
                             ln=(W["g_cm_ln"], W["b_cm_ln"]), bias=W["b_cm_pw2"])
        gfin = W["g_final"] if i == DEPTH - 1 else None
        x = ffn(x, mod, gn[2:3], i, 1, gfin)
    return x, states


def kernel(x_prompt, x_sample, state_pool, state_sconv, cache_k, cache_v, cache_kidx, state_cconv, c_prompt, c_sample, w_mod, b_mod, g_norm, w_ffn_gate, w_ffn_up, w_ffn_down, w_pool, s_pool, w_sc_in, w_sc_conv, w_sc_out, w_attn_in, w_attn_out, w_cm_pw1, b_cm_pw1, w_cm_dw, b_cm_dw, g_cm_ln, b_cm_ln, w_cm_pw2, b_cm_pw2, g_final):
    B, T, _ = x_prompt.shape
    NS, TS, _ = x_sample.shape
    assert D_FF % 512 == 0 and T % 512 == 0 and B + NS <= MOD_ROWS
    assert w_pool.shape[0] == w_sc_in.shape[0] == w_attn_in.shape[0] == w_cm_pw1.shape[0] == 1

    c_all = jnp.concatenate([c_prompt, c_sample, jnp.zeros((MOD_ROWS - B - NS, D), F32)], axis=0)
    mod_all = _mod_call(c_all, w_mod, b_mod)
    mods_p = [mod_all[i, 0:B].reshape(B, 1, 9 * D) for i in range(DEPTH)]
    mods_s = [mod_all[i, B:B + NS].reshape(NS, 1, 9 * D) for i in range(DEPTH)]

    wa = w_attn_in[0].astype(BF)
    W = dict(
        g_norm=g_norm, g_final=g_final.reshape(1, D),
        ffn32=(w_ffn_gate, w_ffn_up, w_ffn_down),
        ffn16={(0, 0): (w_ffn_gate[0, 0].astype(BF), w_ffn_up[0, 0].astype(BF), w_ffn_down[0, 0].astype(BF))},
        w_pool=w_pool[0].astype(BF), s_pool=s_pool,
        w_sc_in=w_sc_in[0].astype(BF), w_sc_conv=w_sc_conv[0], w_sc_out=w_sc_out[0].astype(BF),
        w_attn_main=wa, w_attn_tail=jnp.pad(wa[:, ATTN_MAIN:], ((0, 0), (0, LANE - ATTN_TAIL))),
        w_attn_out=w_attn_out[0].astype(BF),
        w_cm_pw1=w_cm_pw1[0].astype(BF), b_cm_pw1=b_cm_pw1, w_cm_dw=w_cm_dw[0], b_cm_dw=b_cm_dw,
        g_cm_ln=g_cm_ln, b_cm_ln=b_cm_ln, w_cm_pw2=w_cm_pw2[0].astype(BF), b_cm_pw2=b_cm_pw2,
    )

    y_p, st_p = _trunk(x_prompt, mods_p, 1, 512, 512, 512, W, 0, None)
    past = dict(pool=state_pool[0], sconv=state_sconv[0], cconv=state_cconv[0],
                k=cache_k[0].reshape(NS, -1, HEAD_DIM), v=cache_v[0].reshape(NS, -1, HEAD_DIM),
                kidx=cache_kidx[0])
    y_s, st_s = _trunk(x_sample, mods_s, NS, TS, 512, 512, W, cache_k.shape[2], past)

    def kv(a):
        return a.reshape(1, a.shape[0], a.shape[1], KV_HEADS, HEAD_DIM)

    return (y_p, y_s, st_p["pool"][None], st_s["pool"][None], st_p["sconv"][None], st_s["sconv"][None],
            kv(st_p["k"]), kv(st_s["k"]), kv(st_p["v"]), kv(st_s["v"]),
            st_p["kidx"][None], st_s["kidx"][None], st_p["cconv"][None], st_s["cconv"][None])
```
